```python
import math
import jax, jax.numpy as jnp
from jax import lax
import numpy as np

D_MODEL = 1024
BATCH = 2
SEQ = 8192
DEPTH = 1

HEAD_DIM = 64
N_SB_HEADS = 8
N_FOX_HEADS = 8
SB_WIDTH = N_SB_HEADS * HEAD_DIM
FOX_WIDTH = N_FOX_HEADS * HEAD_DIM
Q_BLOCK = 128
D_FF = 2816
CONV_WIDTH = 3
PLE_DIM = 256
RMS_EPS = 1e-6
IN_SPLITS = (SB_WIDTH, SB_WIDTH, SB_WIDTH,
             FOX_WIDTH, FOX_WIDTH, FOX_WIDTH,
             N_FOX_HEADS, D_MODEL, D_MODEL)
IN_COLS = sum(IN_SPLITS)

kernel_name = 'hybrid_stickbreak_fox_convffn_block'


def _rmsnorm(x, g):
    xf = x.astype(jnp.float32)
    y = xf * lax.rsqrt(jnp.mean(xf * xf, axis=-1, keepdims=True) + RMS_EPS)
    return (y * g.astype(jnp.float32)).astype(x.dtype)


def _split_heads(t, n_heads):
    b, s, _ = t.shape
    return t.reshape(b, s, n_heads, HEAD_DIM).transpose(0, 2, 1, 3)


def _merge_heads(t):
    b, h, s, dh = t.shape
    return t.transpose(0, 2, 1, 3).reshape(b, s, h * dh)


def _query_blocks(t):
    b, h, s = t.shape[:3]
    nb = s // Q_BLOCK
    t = t.reshape((b, h, nb, Q_BLOCK) + t.shape[3:])
    return jnp.moveaxis(t, 2, 0)


def _unblock(o):
    nb, b, h, qb, dh = o.shape
    return jnp.moveaxis(o, 0, 2).reshape(b, h, nb * qb, dh)


def _stick_breaking_attention(q, k, v):
    s_len = k.shape[2]
    nb = s_len // Q_BLOCK
    scale = HEAD_DIM ** -0.5
    kpos = jnp.arange(s_len, dtype=jnp.int32)

    def block(args):
        qb, start = args
        qpos = start + jnp.arange(Q_BLOCK, dtype=jnp.int32)
        z = jnp.einsum('bhqd,bhkd->bhqk', qb, k, preferred_element_type=jnp.float32) * scale
        mask = kpos[None, :] < qpos[:, None]
        log_beta = jax.nn.log_sigmoid(z)
        log_one_minus = jnp.where(mask, jax.nn.log_sigmoid(-z), 0.0)
        between = lax.cumsum(log_one_minus, axis=3, reverse=True) - log_one_minus
        weights = jnp.where(mask, jnp.exp(log_beta + between), 0.0)
        return jnp.einsum('bhqk,bhkd->bhqd', weights.astype(v.dtype), v)

    starts = jnp.arange(nb, dtype=jnp.int32) * Q_BLOCK
    return _unblock(lax.map(block, (_query_blocks(q), starts)))


def _forgetting_attention(q, k, v, cum_log_f):
    s_len = k.shape[2]
    nb = s_len // Q_BLOCK
    scale = HEAD_DIM ** -0.5
    kpos = jnp.arange(s_len, dtype=jnp.int32)

    def block(args):
        qb, cq, start = args
        qpos = start + jnp.arange(Q_BLOCK, dtype=jnp.int32)
        logits = jnp.einsum('bhqd,bhkd->bhqk', qb, k, preferred_element_type=jnp.float32) * scale
        logits = logits + cq[..., :, None] - cum_log_f[:, :, None, :]
        mask = kpos[None, :] <= qpos[:, None]
        probs = jax.nn.softmax(jnp.where(mask, logits, -jnp.inf), axis=-1)
        return jnp.einsum('bhqk,bhkd->bhqd', probs.astype(v.dtype), v)

    starts = jnp.arange(nb, dtype=jnp.int32) * Q_BLOCK
    return _unblock(lax.map(block, (_query_blocks(q), _query_blocks(cum_log_f), starts)))


def _causal_depthwise_conv(u, w, b):
    c = u.shape[-1]
    y = lax.conv_general_dilated(
        u, w.astype(u.dtype).reshape(CONV_WIDTH, 1, c),
        window_strides=(1,), padding=[(CONV_WIDTH - 1, 0)],
        dimension_numbers=('NWC', 'WIO', 'NWC'), feature_group_count=c)
    return y + b.astype(u.dtype)


def setup_inputs(seed: int = 0) -> dict:
    key = jax.random.key(seed)
    ks = jax.random.split(key, 20)
    f32 = jnp.float32

    def nrm(k, shape, fan_in):
        return jax.random.normal(k, shape, f32) * (fan_in ** -0.5)

    def gain(k):
        return 1.0 + 0.05 * jax.random.normal(k, (DEPTH, D_MODEL), f32)

    return {
        'x': jax.random.normal(ks[0], (BATCH, SEQ, D_MODEL), f32),
        'p': jax.random.normal(ks[1], (DEPTH, BATCH, SEQ, PLE_DIM), f32),
        'norm_attn_pre': gain(ks[2]),
        'norm_attn_post': gain(ks[3]),
        'w_in': nrm(ks[4], (DEPTH, D_MODEL, IN_COLS), D_MODEL),
        'b_forget': 2.0 + 0.5 * jax.random.normal(ks[5], (DEPTH, N_FOX_HEADS), f32),
        'w_branch_sb': nrm(ks[6], (DEPTH, SB_WIDTH, D_MODEL), SB_WIDTH),
        'w_branch_fox': nrm(ks[7], (DEPTH, FOX_WIDTH, D_MODEL), FOX_WIDTH),
        'w_out': nrm(ks[8], (DEPTH, D_MODEL, D_MODEL), D_MODEL),
        'norm_ffn_pre': gain(ks[9]),
        'norm_ffn_post': gain(ks[10]),
        'w_up': nrm(ks[11], (DEPTH, D_MODEL, 2 * D_FF), D_MODEL),
        'conv_w': nrm(ks[12], (DEPTH, CONV_WIDTH, 2 * D_FF), CONV_WIDTH),
        'conv_b': 0.02 * jax.random.normal(ks[13], (DEPTH, 2 * D_FF), f32),
        'w_down': nrm(ks[14], (DEPTH, D_FF, D_MODEL), D_FF),
        'w_ple': nrm(ks[15], (DEPTH, PLE_DIM, D_MODEL), PLE_DIM),
        'w_ple_gate': nrm(ks[16], (DEPTH, D_MODEL, D_MODEL), D_MODEL),
    }


def reference(x, p, norm_attn_pre, norm_attn_post, w_in, b_forget, w_branch_sb, w_branch_fox,
              w_out, norm_ffn_pre, norm_ffn_post, w_up, conv_w, conv_b, w_down, w_ple, w_ple_gate):
    offsets = list(np.cumsum(IN_SPLITS)[:-1])
    for i in range(DEPTH):
        h = _rmsnorm(x, norm_attn_pre[i])
        proj = h @ w_in[i]
        q_sb, k_sb, v_sb, q_fx, k_fx, v_fx, f_logit, g_sb, g_fx = jnp.split(proj, offsets, axis=-1)

        y_sb = _stick_breaking_attention(_split_heads(q_sb, N_SB_HEADS),
                                         _split_heads(k_sb, N_SB_HEADS),
                                         _split_heads(v_sb, N_SB_HEADS))

        log_f = jax.nn.log_sigmoid(f_logit.astype(jnp.float32) + b_forget[i].astype(jnp.float32))
        cum_log_f = lax.cumsum(log_f, axis=1).transpose(0, 2, 1)
        y_fx = _forgetting_attention(_split_heads(q_fx, N_FOX_HEADS),
                                     _split_heads(k_fx, N_FOX_HEADS),
                                     _split_heads(v_fx, N_FOX_HEADS), cum_log_f)

        z_sb = _merge_heads(y_sb) @ w_branch_sb[i]
        z_fx = _merge_heads(y_fx) @ w_branch_fox[i]
        mixed = jax.nn.sigmoid(g_sb) * z_sb + jax.nn.sigmoid(g_fx) * z_fx
        x = x + _rmsnorm(mixed @ w_out[i], norm_attn_post[i])

        h = _rmsnorm(x, norm_ffn_pre[i])
        u = _causal_depthwise_conv(h @ w_up[i], conv_w[i], conv_b[i])
        u_gate, u_val = jnp.split(u, 2, axis=-1)
        ffn = (jax.nn.gelu(u_gate, approximate=True) * u_val) @ w_down[i]
        x = x + _rmsnorm(ffn, norm_ffn_post[i])

        x = x + jax.nn.sigmoid(x @ w_ple_gate[i]) * (p[i] @ w_ple[i])
    return x
```

```python
import functools

import numpy as np
import jax
import jax.numpy as jnp
from jax import lax
from jax.experimental import pallas as pl
from jax.experimental.pallas import tpu as pltpu

HEAD_DIM = 64
N_HEADS = 8
WIDTH = N_HEADS * HEAD_DIM
HEAD_PAD = 128
D_FF = 2816
CONV_WIDTH = 3
RMS_EPS = 1e-6
SCALE = HEAD_DIM ** -0.5

ATT_BLOCK = 256
FF_CHUNK = 256
HALO = 16
NEG_BIG = -1e30

VMEM_LIMIT = 56 * 1024 * 1024

F32 = jnp.float32
BF16 = jnp.bfloat16


def _dot(a, b):
    return jnp.dot(a, b, preferred_element_type=F32)


def _rms(x, g):
    ms = jnp.mean(x * x, axis=-1, keepdims=True)
    return x * lax.rsqrt(ms + RMS_EPS) * g


def _log_sigmoid(z):
    return jnp.minimum(z, 0.0) - jnp.log(1.0 + jnp.exp(-jnp.abs(z)))


def _sigmoid(z):
    return 1.0 / (1.0 + jnp.exp(-z))


def _split3(c):
    c1 = c.astype(BF16)
    r = c - c1.astype(F32)
    c2 = r.astype(BF16)
    c3 = (r - c2.astype(F32)).astype(BF16)
    return c1, c2, c3


def _const_spec(shape):
    return pl.BlockSpec(shape, lambda *_: (0,) * len(shape))


def _in_proj_kernel(x_ref, g_ref, wsb_ref, wq_ref, wk_ref, wv_ref, wf_ref, bf_ref, wg_ref,
                    selq_ref, selk_ref, oneq_ref, onek_ref,
                    qt_sb_ref, k_sb_ref, vt_sb_ref, qt_fx_ref, k_fx_ref, vt_fx_ref,
                    sg_sb_ref, sg_fx_ref, carry_ref):
    tm = x_ref.shape[1]

    @pl.when(pl.program_id(1) == 0)
    def _():
        carry_ref[...] = jnp.zeros_like(carry_ref)

    h = _rms(x_ref[0], g_ref[...]).astype(BF16)

    logf = _log_sigmoid(_dot(h, wf_ref[...]) + bf_ref[...])
    row = lax.broadcasted_iota(jnp.int32, (tm, tm), 0)
    col = lax.broadcasted_iota(jnp.int32, (tm, tm), 1)
    tri = (col <= row).astype(BF16)
    f1, f2, f3 = _split3(logf)
    c = _dot(tri, f1) + _dot(tri, f2) + _dot(tri, f3) + carry_ref[...]
    carry_ref[...] = c[tm - 1:tm, :]
    c1, c2, c3 = _split3(c)

    def store_transposed_blocks(ref, t):
        for j in range(tm // ATT_BLOCK):
            ref[0, j] = t[:, j * ATT_BLOCK:(j + 1) * ATT_BLOCK].astype(BF16)

    q = _dot(h, wsb_ref[:, 0:WIDTH]) * SCALE
    qt_sb_ref[0] = q.T.astype(BF16)
    k_sb_ref[0] = _dot(h, wsb_ref[:, WIDTH:2 * WIDTH]).astype(BF16)
    store_transposed_blocks(vt_sb_ref, _dot(h, wsb_ref[:, 2 * WIDTH:3 * WIDTH]).T)

    qa = (_dot(h, wq_ref[...]) * SCALE + _dot(c1, selq_ref[0]) + _dot(c2, selq_ref[1])
          + _dot(c3, selq_ref[2]) + oneq_ref[...])
    qt_fx_ref[0] = qa.T.astype(BF16)
    ka = (_dot(h, wk_ref[...]) + _dot(c1, selk_ref[0]) + _dot(c2, selk_ref[1])
          + _dot(c3, selk_ref[2]) + onek_ref[...])
    k_fx_ref[0] = ka.astype(BF16)
    store_transposed_blocks(vt_fx_ref, _dot(h, wv_ref[...]).T)

    d = sg_sb_ref.shape[2]
    sg_sb_ref[0] = _sigmoid(_dot(h, wg_ref[:, 0:d])).astype(BF16)
    sg_fx_ref[0] = _sigmoid(_dot(h, wg_ref[:, d:2 * d])).astype(BF16)


def _in_proj(x, g_pre, w_in, b_forget, tm):
    b, s, d = x.shape
    nblk = s // ATT_BLOCK
    o = np.cumsum([0, WIDTH, WIDTH, WIDTH, WIDTH, WIDTH, WIDTH, N_HEADS, d, d])
    w_sb = w_in[:, o[0]:o[3]].astype(BF16)

    def pad_heads(w):
        w = w.reshape(d, N_HEADS, HEAD_DIM)
        w = jnp.pad(w, ((0, 0), (0, 0), (0, HEAD_PAD - HEAD_DIM)))
        return w.reshape(d, N_HEADS * HEAD_PAD).astype(BF16)

    w_q = pad_heads(w_in[:, o[3]:o[4]])
    w_k = pad_heads(w_in[:, o[4]:o[5]])
    w_v = w_in[:, o[5]:o[6]].astype(BF16)
    w_f = jnp.pad(w_in[:, o[6]:o[7]], ((0, 0), (0, 128 - N_HEADS))).astype(BF16)
    b_f = jnp.pad(b_forget.astype(F32), (0, 128 - N_HEADS)).reshape(1, 128)
    w_g = w_in[:, o[7]:o[9]].astype(BF16)

    selq = np.zeros((3, 128, N_HEADS * HEAD_PAD), np.float32)
    selk = np.zeros((3, 128, N_HEADS * HEAD_PAD), np.float32)
    oneq = np.zeros((1, N_HEADS * HEAD_PAD), np.float32)
    onek = np.zeros((1, N_HEADS * HEAD_PAD), np.float32)
    for hd in range(N_HEADS):
        base = hd * HEAD_PAD + HEAD_DIM
        for i in range(3):
            oneq[0, base + i] = 1.0
            selk[i, hd, base + i] = -1.0
            selq[i, hd, base + 3 + i] = 1.0
            onek[0, base + 3 + i] = 1.0
    selq = jnp.asarray(selq, BF16)
    selk = jnp.asarray(selk, BF16)

    nj = tm // ATT_BLOCK
    out_shape = (
        jax.ShapeDtypeStruct((b, WIDTH, s), BF16),
        jax.ShapeDtypeStruct((b, s, WIDTH), BF16),
        jax.ShapeDtypeStruct((b, nblk, WIDTH, ATT_BLOCK), BF16),
        jax.ShapeDtypeStruct((b, N_HEADS * HEAD_PAD, s), BF16),
        jax.ShapeDtypeStruct((b, s, N_HEADS * HEAD_PAD), BF16),
        jax.ShapeDtypeStruct((b, nblk, WIDTH, ATT_BLOCK), BF16),
        jax.ShapeDtypeStruct((b, s, d), BF16),
        jax.ShapeDtypeStruct((b, s, d), BF16),
    )
    out_specs = (
        pl.BlockSpec((1, WIDTH, tm), lambda bi, i: (bi, 0, i)),
        pl.BlockSpec((1, tm, WIDTH), lambda bi, i: (bi, i, 0)),
        pl.BlockSpec((1, nj, WIDTH, ATT_BLOCK), lambda bi, i: (bi, i, 0, 0)),
        pl.BlockSpec((1, N_HEADS * HEAD_PAD, tm), lambda bi, i: (bi, 0, i)),
        pl.BlockSpec((1, tm, N_HEADS * HEAD_PAD), lambda bi, i: (bi, i, 0)),
        pl.BlockSpec((1, nj, WIDTH, ATT_BLOCK), lambda bi, i: (bi, i, 0, 0)),
        pl.BlockSpec((1, tm, d), lambda bi, i: (bi, i, 0)),
        pl.BlockSpec((1, tm, d), lambda bi, i: (bi, i, 0)),
    )
    in_specs = [
        pl.BlockSpec((1, tm, d), lambda bi, i: (bi, i, 0)),
        _const_spec((1, d)),
        _const_spec(w_sb.shape), _const_spec(w_q.shape), _const_spec(w_k.shape),
        _const_spec(w_v.shape), _const_spec(w_f.shape), _const_spec(b_f.shape),
        _const_spec(w_g.shape), _const_spec(selq.shape), _const_spec(selk.shape),
        _const_spec(oneq.shape), _const_spec(onek.shape),
    ]
    return pl.pallas_call(
        _in_proj_kernel,
        grid=(b, s // tm),
        in_specs=in_specs,
        out_specs=out_specs,
        out_shape=out_shape,
        scratch_shapes=[pltpu.VMEM((1, 128), F32)],
        compiler_params=pltpu.CompilerParams(
            dimension_semantics=("arbitrary", "arbitrary"), vmem_limit_bytes=VMEM_LIMIT),
        name="in_proj",
    )(x, g_pre.reshape(1, d).astype(F32), w_sb, w_q, w_k, w_v, w_f, b_f, w_g,
      selq, selk, jnp.asarray(oneq), jnp.asarray(onek))


def _sb_attn_kernel(qt_ref, k_ref, vt_ref, o_ref):
    t = ATT_BLOCK
    qi = pl.program_id(2)
    qt = qt_ref[0]
    head_row = lax.broadcasted_iota(jnp.int32, qt.shape, 0)
    key = lax.broadcasted_iota(jnp.int32, (t, t), 0)
    qry = lax.broadcasted_iota(jnp.int32, (t, t), 1)
    causal = key < qry
    suffix = (qry > key).astype(BF16)

    accs = []
    for hh in range(2):
        in_head = (head_row < HEAD_DIM) if hh == 0 else (head_row >= HEAD_DIM)
        qth = jnp.where(in_head, qt, jnp.zeros_like(qt))

        def block(kj, carry, acc, diag):
            kblk = k_ref[0, pl.ds(pl.multiple_of(kj * t, t), t), :]
            z = _dot(kblk, qth)
            log_beta = _log_sigmoid(z)
            log_rest = log_beta - z
            if diag:
                log_rest = jnp.where(causal, log_rest, 0.0)
            hi = log_rest.astype(BF16)
            lo = (log_rest - hi.astype(F32)).astype(BF16)
            within = _dot(suffix, hi) + _dot(suffix, lo)
            w = jnp.exp(log_beta + within + carry)
            if diag:
                w = jnp.where(causal, w, 0.0)
            vt = vt_ref[0, kj, hh * HEAD_DIM:(hh + 1) * HEAD_DIM, :]
            acc = acc + _dot(vt, w.astype(BF16))
            carry = carry + within[0:1, :] + log_rest[0:1, :]
            return carry, acc

        carry0 = jnp.zeros((1, t), F32)
        acc0 = jnp.zeros((HEAD_DIM, t), F32)
        carry, acc = block(qi, carry0, acc0, True)

        def body(j, state):
            return block(qi - j, state[0], state[1], False)

        carry, acc = lax.fori_loop(1, qi + 1, body, (carry, acc))
        accs.append(acc)

    o_ref[0] = jnp.concatenate(accs, axis=0).T.astype(o_ref.dtype)


def _sb_attn(qt, k, vt):
    b, _, s = qt.shape
    t = ATT_BLOCK
    nblk = s // t
    return pl.pallas_call(
        _sb_attn_kernel,
        grid=(b, N_HEADS // 2, nblk),
        in_specs=[
            pl.BlockSpec((1, 128, t), lambda bi, hp, qi: (bi, hp, qi)),
            pl.BlockSpec((1, s, 128), lambda bi, hp, qi: (bi, 0, hp)),
            pl.BlockSpec((1, nblk, 128, t), lambda bi, hp, qi: (bi, 0, hp, 0)),
        ],
        out_specs=pl.BlockSpec((1, t, 128), lambda bi, hp, qi: (bi, qi, hp)),
        out_shape=jax.ShapeDtypeStruct((b, s, WIDTH), BF16),
        compiler_params=pltpu.CompilerParams(
            dimension_semantics=("arbitrary", "arbitrary", "arbitrary"),
            vmem_limit_bytes=VMEM_LIMIT),
        name="sb_attn",
    )(qt, k, vt)


def _fox_attn_kernel(qt_ref, k_ref, vt_ref, o_ref):
    t = ATT_BLOCK
    qi = pl.program_id(2)
    key = lax.broadcasted_iota(jnp.int32, (t, t), 0)
    qry = lax.broadcasted_iota(jnp.int32, (t, t), 1)
    causal = key <= qry
    ones = jnp.ones((HALO, t), BF16)

    outs = []
    for hh in range(2):
        qth = qt_ref[0, hh * HEAD_PAD:(hh + 1) * HEAD_PAD, :]

        def block(kj, m, acc, diag):
            kblk = k_ref[0, pl.ds(pl.multiple_of(kj * t, t), t),
                         hh * HEAD_PAD:(hh + 1) * HEAD_PAD]
            logits = _dot(kblk, qth)
            if diag:
                logits = jnp.where(causal, logits, NEG_BIG)
            m_new = jnp.maximum(m, jnp.max(logits, axis=0, keepdims=True))
            p = jnp.exp(logits - m_new)
            vt = vt_ref[0, kj, hh * HEAD_DIM:(hh + 1) * HEAD_DIM, :]
            vt_aug = jnp.concatenate([vt, ones], axis=0)
            acc = acc * jnp.exp(m - m_new) + _dot(vt_aug, p.astype(BF16))
            return m_new, acc

        m0 = jnp.full((1, t), NEG_BIG, F32)
        acc0 = jnp.zeros((HEAD_DIM + HALO, t), F32)
        m, acc = block(qi, m0, acc0, True)

        def body(j, state):
            return block(qi - j, state[0], state[1], False)

        m, acc = lax.fori_loop(1, qi + 1, body, (m, acc))
        outs.append(acc[0:HEAD_DIM, :] / acc[HEAD_DIM:HEAD_DIM + 1, :])

    o_ref[0] = jnp.concatenate(outs, axis=0).T.astype(o_ref.dtype)


def _fox_attn(qt, k, vt):
    b, _, s = qt.shape
    t = ATT_BLOCK
    nblk = s // t
    return pl.pallas_call(
        _fox_attn_kernel,
        grid=(b, N_HEADS // 2, nblk),
        in_specs=[
            pl.BlockSpec((1, 2 * HEAD_PAD, t), lambda bi, hp, qi: (bi, hp, qi)),
            pl.BlockSpec((1, s, 2 * HEAD_PAD), lambda bi, hp, qi: (bi, 0, hp)),
            pl.BlockSpec((1, nblk, 128, t), lambda bi, hp, qi: (bi, 0, hp, 0)),
        ],
        out_specs=pl.BlockSpec((1, t, 128), lambda bi, hp, qi: (bi, qi, hp)),
        out_shape=jax.ShapeDtypeStruct((b, s, WIDTH), BF16),
        compiler_params=pltpu.CompilerParams(
            dimension_semantics=("arbitrary", "arbitrary", "arbitrary"),
            vmem_limit_bytes=VMEM_LIMIT),
        name="fox_attn",
    )(qt, k, vt)


def _post_attn_kernel(x_ref, ysb_ref, yfx_ref, sgsb_ref, sgfx_ref, wbsb_ref, wbfx_ref, wout_ref,
                      gpost_ref, gffn_ref, x1_ref, h2_ref):
    z_sb = _dot(ysb_ref[0], wbsb_ref[...])
    z_fx = _dot(yfx_ref[0], wbfx_ref[...])
    mixed = sgsb_ref[0].astype(F32) * z_sb + sgfx_ref[0].astype(F32) * z_fx
    x1 = x_ref[0] + _rms(_dot(mixed.astype(BF16), wout_ref[...]), gpost_ref[...])
    x1_ref[0] = x1
    h2_ref[0] = _rms(x1, gffn_ref[...]).astype(BF16)


def _post_attn(x, y_sb, y_fx, sg_sb, sg_fx, w_bsb, w_bfx, w_out, g_post, g_ffn, tm):
    b, s, d = x.shape
    tok = lambda w: pl.BlockSpec((1, tm, w), lambda bi, i: (bi, i, 0))
    return pl.pallas_call(
        _post_attn_kernel,
        grid=(b, s // tm),
        in_specs=[tok(d), tok(WIDTH), tok(WIDTH), tok(d), tok(d),
                  _const_spec((WIDTH, d)), _const_spec((WIDTH, d)), _const_spec((d, d)),
                  _const_spec((1, d)), _const_spec((1, d))],
        out_specs=(tok(d), tok(d)),
        out_shape=(jax.ShapeDtypeStruct((b, s, d), F32), jax.ShapeDtypeStruct((b, s, d), BF16)),
        compiler_params=pltpu.CompilerParams(
            dimension_semantics=("arbitrary", "arbitrary"), vmem_limit_bytes=VMEM_LIMIT),
        name="post_attn",
    )(x, y_sb, y_fx, sg_sb, sg_fx, w_bsb.astype(BF16), w_bfx.astype(BF16), w_out.astype(BF16),
      g_post.reshape(1, d).astype(F32), g_ffn.reshape(1, d).astype(F32))


def _gelu_tanh(u):
    return 0.5 * u * (1.0 + jnp.tanh(0.7978845608028654 * (u + 0.044715 * (u * u * u))))


def _ffn_ple_kernel(x1_ref, h2_ref, halo_ref, p_ref, wup_ref, cw_ref, cb_ref, wdn_ref,
                    gpost_ref, wpg_ref, wple_ref, o_ref):
    tm = h2_ref.shape[1]
    n_chunks = wup_ref.shape[0]
    fc = wdn_ref.shape[1]
    halo = jnp.where(pl.program_id(1) > 0, halo_ref[0], jnp.zeros_like(halo_ref[0]))
    h_ext = jnp.concatenate([halo, h2_ref[0]], axis=0)

    acc = jnp.zeros((tm, o_ref.shape[2]), F32)
    for c in range(n_chunks):
        u = _dot(h_ext, wup_ref[c])
        cw = cw_ref[c]
        conv = (cw[0:1, :] * u[HALO - 2:HALO - 2 + tm, :]
                + cw[1:2, :] * u[HALO - 1:HALO - 1 + tm, :]
                + cw[2:3, :] * u[HALO:HALO + tm, :]) + cb_ref[c]
        act = _gelu_tanh(conv[:, 0:fc]) * conv[:, fc:2 * fc]
        acc = acc + _dot(act.astype(BF16), wdn_ref[c])

    x2 = x1_ref[0] + _rms(acc, gpost_ref[...])
    gate = _sigmoid(_dot(x2.astype(BF16), wpg_ref[...]))
    ple = _dot(p_ref[0].astype(BF16), wple_ref[...])
    o_ref[0] = x2 + gate * ple


def _ffn_ple(x1, h2, p, w_up, conv_w, conv_b, w_down, g_post, w_ple, w_ple_gate, tm):
    b, s, d = x1.shape
    fc = FF_CHUNK
    n_chunks = D_FF // fc
    ple_dim = p.shape[-1]

    def chunked(w):
        lead = w.shape[:-1]
        w = w.reshape(lead + (2, n_chunks, fc))
        w = jnp.moveaxis(w, -2, 0)
        return w.reshape((n_chunks,) + lead + (2 * fc,))

    w_up_c = chunked(w_up).astype(BF16)
    cw_c = chunked(conv_w).astype(F32)
    cb_c = chunked(conv_b.reshape(1, 2 * D_FF)).astype(F32)
    w_dn_c = w_down.reshape(n_chunks, fc, d).astype(BF16)

    tok = lambda w: pl.BlockSpec((1, tm, w), lambda bi, i: (bi, i, 0))
    halo_spec = pl.BlockSpec(
        (1, HALO, d), lambda bi, i: (bi, jnp.maximum(i * (tm // HALO) - 1, 0), 0))
    return pl.pallas_call(
        _ffn_ple_kernel,
        grid=(b, s // tm),
        in_specs=[tok(d), tok(d), halo_spec, tok(ple_dim),
                  _const_spec(w_up_c.shape), _const_spec(cw_c.shape), _const_spec(cb_c.shape),
                  _const_spec(w_dn_c.shape), _const_spec((1, d)),
                  _const_spec((d, d)), _const_spec((ple_dim, d))],
        out_specs=tok(d),
        out_shape=jax.ShapeDtypeStruct((b, s, d), F32),
        compiler_params=pltpu.CompilerParams(
            dimension_semantics=("arbitrary", "arbitrary"), vmem_limit_bytes=VMEM_LIMIT),
        name="ffn_ple",
    )(x1, h2, h2, p, w_up_c, cw_c, cb_c, w_dn_c, g_post.reshape(1, d).astype(F32),
      w_ple_gate.astype(BF16), w_ple.astype(BF16))


def _layer(x, p, norm_attn_pre, norm_attn_post, w_in, b_forget, w_branch_sb, w_branch_fox, w_out,
           norm_ffn_pre, norm_ffn_post, w_up, conv_w, conv_b, w_down, w_ple, w_ple_gate):
    qt_sb, k_sb, vt_sb, qt_fx, k_fx, vt_fx, sg_sb, sg_fx = _in_proj(
        x, norm_attn_pre, w_in, b_forget, tm=256)
    y_sb = _sb_attn(qt_sb, k_sb, vt_sb)
    y_fx = _fox_attn(qt_fx, k_fx, vt_fx)
    x1, h2 = _post_attn(x, y_sb, y_fx, sg_sb, sg_fx, w_branch_sb, w_branch_fox, w_out,
                        norm_attn_post, norm_ffn_pre, tm=512)
    return _ffn_ple(x1, h2, p, w_up, conv_w, conv_b, w_down, norm_ffn_post, w_ple, w_ple_gate,
                    tm=256)


def kernel(x, p, norm_attn_pre, norm_attn_post, w_in, b_forget, w_branch_sb, w_branch_fox, w_out,
           norm_ffn_pre, norm_ffn_post, w_up, conv_w, conv_b, w_down, w_ple, w_ple_gate):
    depth = w_in.shape[0]
    for i in range(depth):
        x = _layer(x, p[i], norm_attn_pre[i], norm_attn_post[i], w_in[i], b_forget[i],
                   w_branch_sb[i], w_branch_fox[i], w_out[i], norm_ffn_pre[i], norm_ffn_post[i],
                   w_up[i], conv_w[i], conv_b[i], w_down[i], w_ple[i], w_ple_gate[i])
    return x
```

```python
import functools

import numpy as np
import jax
import jax.numpy as jnp
from jax import lax
from jax.experimental import pallas as pl
from jax.experimental.pallas import tpu as pltpu

HEAD_DIM = 64
N_HEADS = 8
WIDTH = N_HEADS * HEAD_DIM
HEAD_PAD = 128
D_FF = 2816
CONV_WIDTH = 3
RMS_EPS = 1e-6
SCALE = HEAD_DIM ** -0.5

ATT_BLOCK = 256
QRY_BLOCK = 512
FF_CHUNK = 256
HALO = 16
NEG_BIG = -1e30
MAX_INIT = -1e29
LOG2E = 1.4426950408889634

VMEM_LIMIT = 56 * 1024 * 1024

F32 = jnp.float32
BF16 = jnp.bfloat16


def _dot(a, b):
    return jnp.dot(a, b, preferred_element_type=F32)


def _rms(x, g):
    ms = jnp.mean(x * x, axis=-1, keepdims=True)
    return x * lax.rsqrt(ms + RMS_EPS) * g


def _log_sigmoid(z):
    return jnp.minimum(z, 0.0) - jnp.log(1.0 + jnp.exp(-jnp.abs(z)))


def _sigmoid(z):
    return 1.0 / (1.0 + jnp.exp(-z))


def _split3(c):
    c1 = c.astype(BF16)
    r = c - c1.astype(F32)
    c2 = r.astype(BF16)
    c3 = (r - c2.astype(F32)).astype(BF16)
    return c1, c2, c3


def _const_spec(shape):
    return pl.BlockSpec(shape, lambda *_: (0,) * len(shape))


def _in_proj_kernel(x_ref, g_ref, wsb_ref, wq_ref, wk_ref, wv_ref, wf_ref, bf_ref, wg_ref,
                    selq_ref, selk_ref, oneq_ref, onek_ref,
                    qt_sb_ref, k_sb_ref, vt_sb_ref, qt_fx_ref, k_fx_ref, vt_fx_ref,
                    sg_sb_ref, sg_fx_ref, carry_ref):
    tm = x_ref.shape[1]

    @pl.when(pl.program_id(1) == 0)
    def _():
        carry_ref[...] = jnp.zeros_like(carry_ref)

    h = _rms(x_ref[0], g_ref[...]).astype(BF16)

    logf = _log_sigmoid(_dot(h, wf_ref[...]) + bf_ref[...])
    row = lax.broadcasted_iota(jnp.int32, (tm, tm), 0)
    col = lax.broadcasted_iota(jnp.int32, (tm, tm), 1)
    tri = (col <= row).astype(BF16)
    f1, f2, f3 = _split3(logf)
    c = _dot(tri, f1) + _dot(tri, f2) + _dot(tri, f3) + carry_ref[...]
    carry_ref[...] = c[tm - 1:tm, :]
    c1, c2, c3 = _split3(c * LOG2E)

    def store_transposed_blocks(ref, t):
        for j in range(tm // ATT_BLOCK):
            ref[0, j] = t[:, j * ATT_BLOCK:(j + 1) * ATT_BLOCK].astype(BF16)

    q = _dot(h, wsb_ref[:, 0:WIDTH]) * (SCALE * LOG2E)
    qt_sb_ref[0] = q.T.astype(BF16)
    k_sb_ref[0] = _dot(h, wsb_ref[:, WIDTH:2 * WIDTH]).astype(BF16)
    store_transposed_blocks(vt_sb_ref, _dot(h, wsb_ref[:, 2 * WIDTH:3 * WIDTH]).T)

    qa = (_dot(h, wq_ref[...]) * (SCALE * LOG2E) + _dot(c1, selq_ref[0]) + _dot(c2, selq_ref[1])
          + _dot(c3, selq_ref[2]) + oneq_ref[...])
    qt_fx_ref[0] = qa.T.astype(BF16)
    ka = (_dot(h, wk_ref[...]) + _dot(c1, selk_ref[0]) + _dot(c2, selk_ref[1])
          + _dot(c3, selk_ref[2]) + onek_ref[...])
    k_fx_ref[0] = ka.astype(BF16)
    store_transposed_blocks(vt_fx_ref, _dot(h, wv_ref[...]).T)

    d = sg_sb_ref.shape[2]
    sg_sb_ref[0] = _sigmoid(_dot(h, wg_ref[:, 0:d])).astype(BF16)
    sg_fx_ref[0] = _sigmoid(_dot(h, wg_ref[:, d:2 * d])).astype(BF16)


def _in_proj(x, g_pre, w_in, b_forget, tm):
    b, s, d = x.shape
    nblk = s // ATT_BLOCK
    o = np.cumsum([0, WIDTH, WIDTH, WIDTH, WIDTH, WIDTH, WIDTH, N_HEADS, d, d])
    w_sb = w_in[:, o[0]:o[3]].astype(BF16)

    def pad_heads(w):
        w = w.reshape(d, N_HEADS, HEAD_DIM)
        w = jnp.pad(w, ((0, 0), (0, 0), (0, HEAD_PAD - HEAD_DIM)))
        return w.reshape(d, N_HEADS * HEAD_PAD).astype(BF16)

    w_q = pad_heads(w_in[:, o[3]:o[4]])
    w_k = pad_heads(w_in[:, o[4]:o[5]])
    w_v = w_in[:, o[5]:o[6]].astype(BF16)
    w_f = jnp.pad(w_in[:, o[6]:o[7]], ((0, 0), (0, 128 - N_HEADS))).astype(BF16)
    b_f = jnp.pad(b_forget.astype(F32), (0, 128 - N_HEADS)).reshape(1, 128)
    w_g = w_in[:, o[7]:o[9]].astype(BF16)

    selq = np.zeros((3, 128, N_HEADS * HEAD_PAD), np.float32)
    selk = np.zeros((3, 128, N_HEADS * HEAD_PAD), np.float32)
    oneq = np.zeros((1, N_HEADS * HEAD_PAD), np.float32)
    onek = np.zeros((1, N_HEADS * HEAD_PAD), np.float32)
    for hd in range(N_HEADS):
        base = hd * HEAD_PAD + HEAD_DIM
        for i in range(3):
            oneq[0, base + i] = 1.0
            selk[i, hd, base + i] = -1.0
            selq[i, hd, base + 3 + i] = 1.0
            onek[0, base + 3 + i] = 1.0
    selq = jnp.asarray(selq, BF16)
    selk = jnp.asarray(selk, BF16)

    nj = tm // ATT_BLOCK
    out_shape = (
        jax.ShapeDtypeStruct((b, WIDTH, s), BF16),
        jax.ShapeDtypeStruct((b, s, WIDTH), BF16),
        jax.ShapeDtypeStruct((b, nblk, WIDTH, ATT_BLOCK), BF16),
        jax.ShapeDtypeStruct((b, N_HEADS * HEAD_PAD, s), BF16),
        jax.ShapeDtypeStruct((b, s, N_HEADS * HEAD_PAD), BF16),
        jax.ShapeDtypeStruct((b, nblk, WIDTH, ATT_BLOCK), BF16),
        jax.ShapeDtypeStruct((b, s, d), BF16),
        jax.ShapeDtypeStruct((b, s, d), BF16),
    )
    out_specs = (
        pl.BlockSpec((1, WIDTH, tm), lambda bi, i: (bi, 0, i)),
        pl.BlockSpec((1, tm, WIDTH), lambda bi, i: (bi, i, 0)),
        pl.BlockSpec((1, nj, WIDTH, ATT_BLOCK), lambda bi, i: (bi, i, 0, 0)),
        pl.BlockSpec((1, N_HEADS * HEAD_PAD, tm), lambda bi, i: (bi, 0, i)),
        pl.BlockSpec((1, tm, N_HEADS * HEAD_PAD), lambda bi, i: (bi, i, 0)),
        pl.BlockSpec((1, nj, WIDTH, ATT_BLOCK), lambda bi, i: (bi, i, 0, 0)),
        pl.BlockSpec((1, tm, d), lambda bi, i: (bi, i, 0)),
        pl.BlockSpec((1, tm, d), lambda bi, i: (bi, i, 0)),
    )
    in_specs = [
        pl.BlockSpec((1, tm, d), lambda bi, i: (bi, i, 0)),
        _const_spec((1, d)),
        _const_spec(w_sb.shape), _const_spec(w_q.shape), _const_spec(w_k.shape),
        _const_spec(w_v.shape), _const_spec(w_f.shape), _const_spec(b_f.shape),
        _const_spec(w_g.shape), _const_spec(selq.shape), _const_spec(selk.shape),
        _const_spec(oneq.shape), _const_spec(onek.shape),
    ]
    return pl.pallas_call(
        _in_proj_kernel,
        grid=(b, s // tm),
        in_specs=in_specs,
        out_specs=out_specs,
        out_shape=out_shape,
        scratch_shapes=[pltpu.VMEM((1, 128), F32)],
        compiler_params=pltpu.CompilerParams(
            dimension_semantics=("arbitrary", "arbitrary"), vmem_limit_bytes=VMEM_LIMIT),
        name="in_proj",
    )(x, g_pre.reshape(1, d).astype(F32), w_sb, w_q, w_k, w_v, w_f, b_f, w_g,
      selq, selk, jnp.asarray(oneq), jnp.asarray(onek))


def _attn_indices():
    tk, tq = ATT_BLOCK, QRY_BLOCK
    key = lax.broadcasted_iota(jnp.int32, (tk, tq), 0)
    qry = lax.broadcasted_iota(jnp.int32, (tk, tq), 1)
    return key, qry


def _neg_abs(z):
    bits = lax.bitcast_convert_type(z, jnp.uint32) | jnp.uint32(0x80000000)
    return lax.bitcast_convert_type(bits, F32)


def _sb_attn_kernel(qt_ref, k_ref, vt_ref, o_ref, hi_ref, lo_ref, lb_ref, t_ref):
    tk, tq = ATT_BLOCK, QRY_BLOCK
    per_q = tq // tk
    assert per_q == 2, "the two scratch slots assume an even number of key blocks per query block"
    qi = pl.program_id(2)
    last = (qi + 1) * per_q - 1
    qt = qt_ref[0]
    head_row = lax.broadcasted_iota(jnp.int32, qt.shape, 0)
    zero = jnp.zeros_like(qt)
    qth = (jnp.where(head_row < HEAD_DIM, qt, zero),
           jnp.where(head_row >= HEAD_DIM, qt, zero))
    key, qry = _attn_indices()
    row = lax.broadcasted_iota(jnp.int32, (tk, tk), 0)
    col = lax.broadcasted_iota(jnp.int32, (tk, tk), 1)
    suffix = (col > row).astype(BF16)

    def stage_a(n, slot, diag):
        kj = last - n
        kblk = k_ref[0, pl.ds(pl.multiple_of(kj * tk, tk), tk), :]
        if diag:
            causal = kj * tk + key < qi * tq + qry
        first_rows = []
        for hh in range(2):
            z = _dot(kblk, qth[hh])
            log_beta = jnp.minimum(z, 0.0) - jnp.log2(1.0 + jnp.exp2(_neg_abs(z)))
            log_rest = log_beta - z
            if diag:
                log_rest = jnp.where(causal, log_rest, 0.0)
                log_beta = jnp.where(causal, log_beta, NEG_BIG)
            hi = log_rest.astype(BF16)
            hi_ref[slot, hh] = hi
            lo_ref[slot, hh] = (log_rest - hi.astype(F32)).astype(BF16)
            lb_ref[slot, hh] = log_beta
            first_rows.append(log_rest[0:1, :])
        return tuple(first_rows)

    def stage_b(slot, first_rows):
        col_sums = []
        for hh in range(2):
            within = _dot(suffix, hi_ref[slot, hh]) + _dot(suffix, lo_ref[slot, hh])
            t_ref[slot, hh] = lb_ref[slot, hh] + within
            col_sums.append(within[0:1, :] + first_rows[hh])
        return tuple(col_sums)

    def stage_c(n, slot, col_sums, state):
        kj = last - n
        new_state = []
        for hh in range(2):
            carry, acc = state[hh]
            w = jnp.exp2(t_ref[slot, hh] + carry)
            vt = vt_ref[0, kj, hh * HEAD_DIM:(hh + 1) * HEAD_DIM, :]
            acc = acc + _dot(vt, w.astype(BF16))
            new_state.append((carry + col_sums[hh], acc))
        return tuple(new_state)

    state = tuple((jnp.zeros((1, tq), F32), jnp.zeros((HEAD_DIM, tq), F32)) for _ in range(2))
    rows0 = stage_a(0, 0, True)
    rows1 = stage_a(1, 1, True)
    sums0 = stage_b(0, rows0)

    def trip(p, carried):
        rows1, sums0, state = carried
        n = 2 * p
        rows0 = stage_a(n, 0, False)
        sums1 = stage_b(1, rows1)
        state = stage_c(n - 2, 0, sums0, state)
        rows1 = stage_a(n + 1, 1, False)
        sums0 = stage_b(0, rows0)
        state = stage_c(n - 1, 1, sums1, state)
        return rows1, sums0, state

    rows1, sums0, state = lax.fori_loop(1, qi + 1, trip, (rows1, sums0, state))
    sums1 = stage_b(1, rows1)
    state = stage_c(last - 1, 0, sums0, state)
    state = stage_c(last, 1, sums1, state)
    o_ref[0] = jnp.concatenate([st[1] for st in state], axis=0).T.astype(o_ref.dtype)


def _sb_scratch():
    tk, tq = ATT_BLOCK, QRY_BLOCK
    return [pltpu.VMEM((2, 2, tk, tq), BF16), pltpu.VMEM((2, 2, tk, tq), BF16),
            pltpu.VMEM((2, 2, tk, tq), F32), pltpu.VMEM((2, 2, tk, tq), F32)]


def _attn_call(kernel_fn, name, qt, k, vt, head_rows, scratch):
    b, _, s = qt.shape
    tk, tq = ATT_BLOCK, QRY_BLOCK
    return pl.pallas_call(
        kernel_fn,
        grid=(b, N_HEADS // 2, s // tq),
        in_specs=[
            pl.BlockSpec((1, 2 * head_rows, tq), lambda bi, hp, qi: (bi, hp, qi)),
            pl.BlockSpec((1, s, 2 * head_rows), lambda bi, hp, qi: (bi, 0, hp)),
            pl.BlockSpec((1, s // tk, 2 * HEAD_DIM, tk), lambda bi, hp, qi: (bi, 0, hp, 0)),
        ],
        out_specs=pl.BlockSpec((1, tq, 2 * HEAD_DIM), lambda bi, hp, qi: (bi, qi, hp)),
        out_shape=jax.ShapeDtypeStruct((b, s, WIDTH), BF16),
        scratch_shapes=scratch,
        compiler_params=pltpu.CompilerParams(
            dimension_semantics=("arbitrary", "arbitrary", "arbitrary"),
            vmem_limit_bytes=VMEM_LIMIT),
        name=name,
    )(qt, k, vt)


def _fox_attn_kernel(qt_ref, k_ref, vt_ref, o_ref, s_ref):
    tk, tq = ATT_BLOCK, QRY_BLOCK
    per_q = tq // tk
    assert per_q == 2, "the two scratch slots assume an even number of key blocks per query block"
    qi = pl.program_id(2)
    last = (qi + 1) * per_q - 1
    key, qry = _attn_indices()
    ones = jnp.ones((HALO, tk), BF16)
    qth = tuple(qt_ref[0, hh * HEAD_PAD:(hh + 1) * HEAD_PAD, :] for hh in range(2))

    def stage_a(n, slot, diag):
        kj = last - n
        rows = pl.ds(pl.multiple_of(kj * tk, tk), tk)
        if diag:
            causal = kj * tk + key <= qi * tq + qry
        maxes = []
        for hh in range(2):
            kblk = k_ref[0, rows, hh * HEAD_PAD:(hh + 1) * HEAD_PAD]
            logits = _dot(kblk, qth[hh])
            if diag:
                logits = jnp.where(causal, logits, NEG_BIG)
            s_ref[slot, hh] = logits
            maxes.append(jnp.max(logits, axis=0, keepdims=True))
        return tuple(maxes)

    def stage_b(n, slot, maxes, state):
        kj = last - n
        new_state = []
        for hh in range(2):
            m, acc = state[hh]
            m_new = jnp.maximum(m, maxes[hh])
            p = jnp.exp2(s_ref[slot, hh] - m_new)
            vt = vt_ref[0, kj, hh * HEAD_DIM:(hh + 1) * HEAD_DIM, :]
            vt_aug = jnp.concatenate([vt, ones], axis=0)
            acc = acc * jnp.exp2(m - m_new) + _dot(vt_aug, p.astype(BF16))
            new_state.append((m_new, acc))
        return tuple(new_state)

    state = tuple((jnp.full((1, tq), MAX_INIT, F32), jnp.zeros((HEAD_DIM + HALO, tq), F32))
                  for _ in range(2))
    max0 = stage_a(0, 0, True)
    max1 = stage_a(1, 1, True)
    state = stage_b(0, 0, max0, state)

    def trip(p, carried):
        max1, state = carried
        n = 2 * p
        max0 = stage_a(n, 0, False)
        state = stage_b(n - 1, 1, max1, state)
        max1 = stage_a(n + 1, 1, False)
        state = stage_b(n, 0, max0, state)
        return max1, state

    max1, state = lax.fori_loop(1, qi + 1, trip, (max1, state))
    state = stage_b(2 * qi + 1, 1, max1, state)
    outs = [acc[0:HEAD_DIM, :] / acc[HEAD_DIM:HEAD_DIM + 1, :] for _, acc in state]
    o_ref[0] = jnp.concatenate(outs, axis=0).T.astype(o_ref.dtype)


def _fox_scratch():
    return [pltpu.VMEM((2, 2, ATT_BLOCK, QRY_BLOCK), F32)]


def _post_attn_kernel(x_ref, ysb_ref, yfx_ref, sgsb_ref, sgfx_ref, wbsb_ref, wbfx_ref, wout_ref,
                      gpost_ref, gffn_ref, x1_ref, h2_ref):
    z_sb = _dot(ysb_ref[0], wbsb_ref[...])
    z_fx = _dot(yfx_ref[0], wbfx_ref[...])
    mixed = sgsb_ref[0].astype(F32) * z_sb + sgfx_ref[0].astype(F32) * z_fx
    x1 = x_ref[0] + _rms(_dot(mixed.astype(BF16), wout_ref[...]), gpost_ref[...])
    x1_ref[0] = x1
    h2_ref[0] = _rms(x1, gffn_ref[...]).astype(BF16)


def _post_attn(x, y_sb, y_fx, sg_sb, sg_fx, w_bsb, w_bfx, w_out, g_post, g_ffn, tm):
    b, s, d = x.shape
    tok = lambda w: pl.BlockSpec((1, tm, w), lambda bi, i: (bi, i, 0))
    return pl.pallas_call(
        _post_attn_kernel,
        grid=(b, s // tm),
        in_specs=[tok(d), tok(WIDTH), tok(WIDTH), tok(d), tok(d),
                  _const_spec((WIDTH, d)), _const_spec((WIDTH, d)), _const_spec((d, d)),
                  _const_spec((1, d)), _const_spec((1, d))],
        out_specs=(tok(d), tok(d)),
        out_shape=(jax.ShapeDtypeStruct((b, s, d), F32), jax.ShapeDtypeStruct((b, s, d), BF16)),
        compiler_params=pltpu.CompilerParams(
            dimension_semantics=("arbitrary", "arbitrary"), vmem_limit_bytes=VMEM_LIMIT),
        name="post_attn",
    )(x, y_sb, y_fx, sg_sb, sg_fx, w_bsb.astype(BF16), w_bfx.astype(BF16), w_out.astype(BF16),
      g_post.reshape(1, d).astype(F32), g_ffn.reshape(1, d).astype(F32))


def _gelu_tanh(u):
    return 0.5 * u * (1.0 + jnp.tanh(0.7978845608028654 * (u + 0.044715 * (u * u * u))))


def _ffn_ple_kernel(x1_ref, h2_ref, halo_ref, p_ref, wup_ref, cw_ref, cb_ref, wdn_ref,
                    gpost_ref, wpg_ref, wple_ref, o_ref):
    tm = h2_ref.shape[1]
    n_chunks = wup_ref.shape[0]
    fc = wdn_ref.shape[1]
    halo = jnp.where(pl.program_id(1) > 0, halo_ref[0], jnp.zeros_like(halo_ref[0]))
    h_ext = jnp.concatenate([halo, h2_ref[0]], axis=0)

    acc = jnp.zeros((tm, o_ref.shape[2]), F32)
    for c in range(n_chunks):
        u = _dot(h_ext, wup_ref[c])
        cw = cw_ref[c]
        conv = (cw[0:1, :] * u[HALO - 2:HALO - 2 + tm, :]
                + cw[1:2, :] * u[HALO - 1:HALO - 1 + tm, :]
                + cw[2:3, :] * u[HALO:HALO + tm, :]) + cb_ref[c]
        act = _gelu_tanh(conv[:, 0:fc]) * conv[:, fc:2 * fc]
        acc = acc + _dot(act.astype(BF16), wdn_ref[c])

    x2 = x1_ref[0] + _rms(acc, gpost_ref[...])
    gate = _sigmoid(_dot(x2.astype(BF16), wpg_ref[...]))
    ple = _dot(p_ref[0].astype(BF16), wple_ref[...])
    o_ref[0] = x2 + gate * ple


def _ffn_ple(x1, h2, p, w_up, conv_w, conv_b, w_down, g_post, w_ple, w_ple_gate, tm):
    b, s, d = x1.shape
    fc = FF_CHUNK
    n_chunks = D_FF // fc
    ple_dim = p.shape[-1]

    def chunked(w):
        lead = w.shape[:-1]
        w = w.reshape(lead + (2, n_chunks, fc))
        w = jnp.moveaxis(w, -2, 0)
        return w.reshape((n_chunks,) + lead + (2 * fc,))

    w_up_c = chunked(w_up).astype(BF16)
    cw_c = chunked(conv_w).astype(F32)
    cb_c = chunked(conv_b.reshape(1, 2 * D_FF)).astype(F32)
    w_dn_c = w_down.reshape(n_chunks, fc, d).astype(BF16)

    tok = lambda w: pl.BlockSpec((1, tm, w), lambda bi, i: (bi, i, 0))
    halo_spec = pl.BlockSpec(
        (1, HALO, d), lambda bi, i: (bi, jnp.maximum(i * (tm // HALO) - 1, 0), 0))
    return pl.pallas_call(
        _ffn_ple_kernel,
        grid=(b, s // tm),
        in_specs=[tok(d), tok(d), halo_spec, tok(ple_dim),
                  _const_spec(w_up_c.shape), _const_spec(cw_c.shape), _const_spec(cb_c.shape),
                  _const_spec(w_dn_c.shape), _const_spec((1, d)),
                  _const_spec((d, d)), _const_spec((ple_dim, d))],
        out_specs=tok(d),
        out_shape=jax.ShapeDtypeStruct((b, s, d), F32),
        compiler_params=pltpu.CompilerParams(
            dimension_semantics=("arbitrary", "arbitrary"), vmem_limit_bytes=VMEM_LIMIT),
        name="ffn_ple",
    )(x1, h2, h2, p, w_up_c, cw_c, cb_c, w_dn_c, g_post.reshape(1, d).astype(F32),
      w_ple_gate.astype(BF16), w_ple.astype(BF16))


def _layer(x, p, norm_attn_pre, norm_attn_post, w_in, b_forget, w_branch_sb, w_branch_fox, w_out,
           norm_ffn_pre, norm_ffn_post, w_up, conv_w, conv_b, w_down, w_ple, w_ple_gate):
    qt_sb, k_sb, vt_sb, qt_fx, k_fx, vt_fx, sg_sb, sg_fx = _in_proj(
        x, norm_attn_pre, w_in, b_forget, tm=256)
    y_sb = _attn_call(_sb_attn_kernel, "sb_attn", qt_sb, k_sb, vt_sb, HEAD_DIM, _sb_scratch())
    y_fx = _attn_call(_fox_attn_kernel, "fox_attn", qt_fx, k_fx, vt_fx, HEAD_PAD, _fox_scratch())
    x1, h2 = _post_attn(x, y_sb, y_fx, sg_sb, sg_fx, w_branch_sb, w_branch_fox, w_out,
                        norm_attn_post, norm_ffn_pre, tm=512)
    return _ffn_ple(x1, h2, p, w_up, conv_w, conv_b, w_down, norm_ffn_post, w_ple, w_ple_gate,
                    tm=256)


def kernel(x, p, norm_attn_pre, norm_attn_post, w_in, b_forget, w_branch_sb, w_branch_fox, w_out,
           norm_ffn_pre, norm_ffn_post, w_up, conv_w, conv_b, w_down, w_ple, w_ple_gate):
    depth = w_in.shape[0]
    for i in range(depth):
        x = _layer(x, p[i], norm_attn_pre[i], norm_attn_post[i], w_in[i], b_forget[i],
                   w_branch_sb[i], w_branch_fox[i], w_out[i], norm_ffn_pre[i], norm_ffn_post[i],
                   w_up[i], conv_w[i], conv_b[i], w_down[i], w_ple[i], w_ple_gate[i])
    return x
```

```python
import functools

import numpy as np
import jax
import jax.numpy as jnp
from jax import lax
from jax.experimental import pallas as pl
from jax.experimental.pallas import tpu as pltpu

HEAD_DIM = 64
N_HEADS = 8
WIDTH = N_HEADS * HEAD_DIM
HEAD_PAD = 128
D_FF = 2816
CONV_WIDTH = 3
RMS_EPS = 1e-6
SCALE = HEAD_DIM ** -0.5

ATT_BLOCK = 256
QRY_BLOCK = 512
FF_CHUNK = 256
HALO = 16
NEG_BIG = -1e30
MAX_INIT = -1e29
LOG2E = 1.4426950408889634

VMEM_LIMIT = 56 * 1024 * 1024

F32 = jnp.float32
BF16 = jnp.bfloat16


def _dot(a, b):
    return jnp.dot(a, b, preferred_element_type=F32)


def _rms(x, g):
    ms = jnp.mean(x * x, axis=-1, keepdims=True)
    return x * lax.rsqrt(ms + RMS_EPS) * g


def _log_sigmoid(z):
    return jnp.minimum(z, 0.0) - jnp.log(1.0 + jnp.exp(-jnp.abs(z)))


def _sigmoid(z):
    return 1.0 / (1.0 + jnp.exp(-z))


def _split3(c):
    c1 = c.astype(BF16)
    r = c - c1.astype(F32)
    c2 = r.astype(BF16)
    c3 = (r - c2.astype(F32)).astype(BF16)
    return c1, c2, c3


def _const_spec(shape):
    return pl.BlockSpec(shape, lambda *_: (0,) * len(shape), pipeline_mode=pl.Buffered(1))


def _in_proj_kernel(x_ref, g_ref, wsb_ref, wq_ref, wk_ref, wv_ref, wf_ref, bf_ref, wg_ref,
                    selq_ref, selk_ref, oneq_ref, onek_ref,
                    qt_sb_ref, k_sb_ref, vt_sb_ref, qt_fx_ref, k_fx_ref, vt_fx_ref,
                    sg_sb_ref, sg_fx_ref, carry_ref):
    tm = x_ref.shape[1]

    @pl.when(pl.program_id(1) == 0)
    def _():
        carry_ref[...] = jnp.zeros_like(carry_ref)

    h = _rms(x_ref[0], g_ref[...]).astype(BF16)

    logf = _log_sigmoid(_dot(h, wf_ref[...]) + bf_ref[...])
    row = lax.broadcasted_iota(jnp.int32, (tm, tm), 0)
    col = lax.broadcasted_iota(jnp.int32, (tm, tm), 1)
    tri = (col <= row).astype(BF16)
    f1, f2, f3 = _split3(logf)
    c = _dot(tri, f1) + _dot(tri, f2) + _dot(tri, f3) + carry_ref[...]
    carry_ref[...] = c[tm - 1:tm, :]
    c1, c2, c3 = _split3(c * LOG2E)

    def store_transposed_blocks(ref, t):
        for j in range(tm // ATT_BLOCK):
            ref[0, j] = t[:, j * ATT_BLOCK:(j + 1) * ATT_BLOCK].astype(BF16)

    q = _dot(h, wsb_ref[:, 0:WIDTH]) * (SCALE * LOG2E)
    qt_sb_ref[0] = q.T.astype(BF16)
    k_sb_ref[0] = _dot(h, wsb_ref[:, WIDTH:2 * WIDTH]).astype(BF16)
    store_transposed_blocks(vt_sb_ref, _dot(h, wsb_ref[:, 2 * WIDTH:3 * WIDTH]).T)

    qa = (_dot(h, wq_ref[...]) * (SCALE * LOG2E) + _dot(c1, selq_ref[0]) + _dot(c2, selq_ref[1])
          + _dot(c3, selq_ref[2]) + oneq_ref[...])
    qt_fx_ref[0] = qa.T.astype(BF16)
    ka = (_dot(h, wk_ref[...]) + _dot(c1, selk_ref[0]) + _dot(c2, selk_ref[1])
          + _dot(c3, selk_ref[2]) + onek_ref[...])
    k_fx_ref[0] = ka.astype(BF16)
    store_transposed_blocks(vt_fx_ref, _dot(h, wv_ref[...]).T)

    d = sg_sb_ref.shape[2]
    sg_sb_ref[0] = _sigmoid(_dot(h, wg_ref[:, 0:d])).astype(BF16)
    sg_fx_ref[0] = _sigmoid(_dot(h, wg_ref[:, d:2 * d])).astype(BF16)


def _in_proj(x, g_pre, w_in, b_forget, tm):
    b, s, d = x.shape
    nblk = s // ATT_BLOCK
    o = np.cumsum([0, WIDTH, WIDTH, WIDTH, WIDTH, WIDTH, WIDTH, N_HEADS, d, d])
    w_sb = w_in[:, o[0]:o[3]].astype(BF16)

    def pad_heads(w):
        w = w.reshape(d, N_HEADS, HEAD_DIM)
        w = jnp.pad(w, ((0, 0), (0, 0), (0, HEAD_PAD - HEAD_DIM)))
        return w.reshape(d, N_HEADS * HEAD_PAD).astype(BF16)

    w_q = pad_heads(w_in[:, o[3]:o[4]])
    w_k = pad_heads(w_in[:, o[4]:o[5]])
    w_v = w_in[:, o[5]:o[6]].astype(BF16)
    w_f = jnp.pad(w_in[:, o[6]:o[7]], ((0, 0), (0, 128 - N_HEADS))).astype(BF16)
    b_f = jnp.pad(b_forget.astype(F32), (0, 128 - N_HEADS)).reshape(1, 128)
    w_g = w_in[:, o[7]:o[9]].astype(BF16)

    selq = np.zeros((3, 128, N_HEADS * HEAD_PAD), np.float32)
    selk = np.zeros((3, 128, N_HEADS * HEAD_PAD), np.float32)
    oneq = np.zeros((1, N_HEADS * HEAD_PAD), np.float32)
    onek = np.zeros((1, N_HEADS * HEAD_PAD), np.float32)
    for hd in range(N_HEADS):
        base = hd * HEAD_PAD + HEAD_DIM
        for i in range(3):
            oneq[0, base + i] = 1.0
            selk[i, hd, base + i] = -1.0
            selq[i, hd, base + 3 + i] = 1.0
            onek[0, base + 3 + i] = 1.0
    selq = jnp.asarray(selq, BF16)
    selk = jnp.asarray(selk, BF16)

    nj = tm // ATT_BLOCK
    out_shape = (
        jax.ShapeDtypeStruct((b, WIDTH, s), BF16),
        jax.ShapeDtypeStruct((b, s, WIDTH), BF16),
        jax.ShapeDtypeStruct((b, nblk, WIDTH, ATT_BLOCK), BF16),
        jax.ShapeDtypeStruct((b, N_HEADS * HEAD_PAD, s), BF16),
        jax.ShapeDtypeStruct((b, s, N_HEADS * HEAD_PAD), BF16),
        jax.ShapeDtypeStruct((b, nblk, WIDTH, ATT_BLOCK), BF16),
        jax.ShapeDtypeStruct((b, s, d), BF16),
        jax.ShapeDtypeStruct((b, s, d), BF16),
    )
    out_specs = (
        pl.BlockSpec((1, WIDTH, tm), lambda bi, i: (bi, 0, i)),
        pl.BlockSpec((1, tm, WIDTH), lambda bi, i: (bi, i, 0)),
        pl.BlockSpec((1, nj, WIDTH, ATT_BLOCK), lambda bi, i: (bi, i, 0, 0)),
        pl.BlockSpec((1, N_HEADS * HEAD_PAD, tm), lambda bi, i: (bi, 0, i)),
        pl.BlockSpec((1, tm, N_HEADS * HEAD_PAD), lambda bi, i: (bi, i, 0)),
        pl.BlockSpec((1, nj, WIDTH, ATT_BLOCK), lambda bi, i: (bi, i, 0, 0)),
        pl.BlockSpec((1, tm, d), lambda bi, i: (bi, i, 0)),
        pl.BlockSpec((1, tm, d), lambda bi, i: (bi, i, 0)),
    )
    in_specs = [
        pl.BlockSpec((1, tm, d), lambda bi, i: (bi, i, 0)),
        _const_spec((1, d)),
        _const_spec(w_sb.shape), _const_spec(w_q.shape), _const_spec(w_k.shape),
        _const_spec(w_v.shape), _const_spec(w_f.shape), _const_spec(b_f.shape),
        _const_spec(w_g.shape), _const_spec(selq.shape), _const_spec(selk.shape),
        _const_spec(oneq.shape), _const_spec(onek.shape),
    ]
    return pl.pallas_call(
        _in_proj_kernel,
        grid=(b, s // tm),
        in_specs=in_specs,
        out_specs=out_specs,
        out_shape=out_shape,
        scratch_shapes=[pltpu.VMEM((1, 128), F32)],
        compiler_params=pltpu.CompilerParams(
            dimension_semantics=("arbitrary", "arbitrary"), vmem_limit_bytes=VMEM_LIMIT),
        name="in_proj",
    )(x, g_pre.reshape(1, d).astype(F32), w_sb, w_q, w_k, w_v, w_f, b_f, w_g,
      selq, selk, jnp.asarray(oneq), jnp.asarray(onek))


def _diag_offsets():
    tk, tq = ATT_BLOCK, QRY_BLOCK
    offsets = (lax.broadcasted_iota(jnp.int32, (tk, tq), 0)
               - lax.broadcasted_iota(jnp.int32, (tk, tq), 1))
    return offsets, lambda n: (n + 1 - tq // tk) * tk


def _neg_abs(z):
    bits = lax.bitcast_convert_type(z, jnp.uint32) | jnp.uint32(0x80000000)
    return lax.bitcast_convert_type(bits, F32)


def _sb_attn_kernel(qt_ref, k_ref, vt_ref, o_ref, lr_ref, lb_ref, t_ref):
    tk, tq = ATT_BLOCK, QRY_BLOCK
    per_q = tq // tk
    assert per_q == 2, "the two scratch slots assume an even number of key blocks per query block"
    qi = pl.program_id(2)
    last = (qi + 1) * per_q - 1
    qt = qt_ref[0]
    head_row = lax.broadcasted_iota(jnp.int32, qt.shape, 0)
    zero = jnp.zeros_like(qt)
    qth = (jnp.where(head_row < HEAD_DIM, qt, zero),
           jnp.where(head_row >= HEAD_DIM, qt, zero))
    offsets, shift = _diag_offsets()
    row = lax.broadcasted_iota(jnp.int32, (tk, tk), 0)
    col = lax.broadcasted_iota(jnp.int32, (tk, tk), 1)
    suffix = (col > row).astype(BF16)

    def stage_a(n, slot, diag):
        kj = last - n
        kblk = k_ref[0, pl.ds(pl.multiple_of(kj * tk, tk), tk), :]
        if diag:
            causal = offsets < shift(n)
        first_rows = []
        for hh in range(2):
            z = _dot(kblk, qth[hh])
            log_beta = jnp.minimum(z, 0.0) - jnp.log2(1.0 + jnp.exp2(_neg_abs(z)))
            log_rest = log_beta - z
            if diag:
                log_rest = jnp.where(causal, log_rest, 0.0)
                log_beta = jnp.where(causal, log_beta, NEG_BIG)
            lr_ref[slot, hh] = log_rest.astype(BF16)
            lb_ref[slot, hh] = log_beta
            first_rows.append(log_rest[0:1, :])
        return tuple(first_rows)

    def stage_b(slot, first_rows):
        col_sums = []
        for hh in range(2):
            within = _dot(suffix, lr_ref[slot, hh])
            t_ref[slot, hh] = lb_ref[slot, hh] + within
            col_sums.append(within[0:1, :] + first_rows[hh])
        return tuple(col_sums)

    def stage_c(n, slot, col_sums, state):
        kj = last - n
        new_state = []
        for hh in range(2):
            carry, acc = state[hh]
            w = jnp.exp2(t_ref[slot, hh] + carry)
            vt = vt_ref[0, kj, hh * HEAD_DIM:(hh + 1) * HEAD_DIM, :]
            acc = acc + _dot(vt, w.astype(BF16))
            new_state.append((carry + col_sums[hh], acc))
        return tuple(new_state)

    state = tuple((jnp.zeros((1, tq), F32), jnp.zeros((HEAD_DIM, tq), F32)) for _ in range(2))
    rows0 = stage_a(0, 0, True)
    rows1 = stage_a(1, 1, True)
    sums0 = stage_b(0, rows0)

    def trip(p, carried):
        rows1, sums0, state = carried
        n = 2 * p
        rows0 = stage_a(n, 0, False)
        sums1 = stage_b(1, rows1)
        state = stage_c(n - 2, 0, sums0, state)
        rows1 = stage_a(n + 1, 1, False)
        sums0 = stage_b(0, rows0)
        state = stage_c(n - 1, 1, sums1, state)
        return rows1, sums0, state

    rows1, sums0, state = lax.fori_loop(1, qi + 1, trip, (rows1, sums0, state))
    sums1 = stage_b(1, rows1)
    state = stage_c(last - 1, 0, sums0, state)
    state = stage_c(last, 1, sums1, state)
    o_ref[0] = jnp.concatenate([st[1] for st in state], axis=0).T.astype(o_ref.dtype)


def _sb_scratch():
    tk, tq = ATT_BLOCK, QRY_BLOCK
    return [pltpu.VMEM((2, 2, tk, tq), BF16),
            pltpu.VMEM((2, 2, tk, tq), F32), pltpu.VMEM((2, 2, tk, tq), F32)]


def _attn_call(kernel_fn, name, qt, k, vt, head_rows, scratch):
    b, _, s = qt.shape
    tk, tq = ATT_BLOCK, QRY_BLOCK
    return pl.pallas_call(
        kernel_fn,
        grid=(b, N_HEADS // 2, s // tq),
        in_specs=[
            pl.BlockSpec((1, 2 * head_rows, tq), lambda bi, hp, qi: (bi, hp, qi)),
            pl.BlockSpec((1, s, 2 * head_rows), lambda bi, hp, qi: (bi, 0, hp)),
            pl.BlockSpec((1, s // tk, 2 * HEAD_DIM, tk), lambda bi, hp, qi: (bi, 0, hp, 0)),
        ],
        out_specs=pl.BlockSpec((1, tq, 2 * HEAD_DIM), lambda bi, hp, qi: (bi, qi, hp)),
        out_shape=jax.ShapeDtypeStruct((b, s, WIDTH), BF16),
        scratch_shapes=scratch,
        compiler_params=pltpu.CompilerParams(
            dimension_semantics=("arbitrary", "arbitrary", "arbitrary"),
            vmem_limit_bytes=VMEM_LIMIT),
        name=name,
    )(qt, k, vt)


def _fox_attn_kernel(qt_ref, k_ref, vt_ref, o_ref, s_ref):
    tk, tq = ATT_BLOCK, QRY_BLOCK
    per_q = tq // tk
    assert per_q == 2, "the two scratch slots assume an even number of key blocks per query block"
    qi = pl.program_id(2)
    last = (qi + 1) * per_q - 1
    offsets, shift = _diag_offsets()
    ones = jnp.ones((HALO, tk), BF16)
    qth = tuple(qt_ref[0, hh * HEAD_PAD:(hh + 1) * HEAD_PAD, :] for hh in range(2))

    def stage_a(n, slot, diag):
        kj = last - n
        rows = pl.ds(pl.multiple_of(kj * tk, tk), tk)
        if diag:
            causal = offsets <= shift(n)
        maxes = []
        for hh in range(2):
            kblk = k_ref[0, rows, hh * HEAD_PAD:(hh + 1) * HEAD_PAD]
            logits = _dot(kblk, qth[hh])
            if diag:
                logits = jnp.where(causal, logits, NEG_BIG)
            s_ref[slot, hh] = logits
            maxes.append(jnp.max(logits, axis=0, keepdims=True))
        return tuple(maxes)

    def stage_b(n, slot, maxes, state):
        kj = last - n
        new_state = []
        for hh in range(2):
            m, acc = state[hh]
            m_new = jnp.maximum(m, maxes[hh])
            p = jnp.exp2(s_ref[slot, hh] - m_new)
            vt = vt_ref[0, kj, hh * HEAD_DIM:(hh + 1) * HEAD_DIM, :]
            vt_aug = jnp.concatenate([vt, ones], axis=0)
            acc = acc * jnp.exp2(m - m_new) + _dot(vt_aug, p.astype(BF16))
            new_state.append((m_new, acc))
        return tuple(new_state)

    state = tuple((jnp.full((1, tq), MAX_INIT, F32), jnp.zeros((HEAD_DIM + HALO, tq), F32))
                  for _ in range(2))
    max0 = stage_a(0, 0, True)
    max1 = stage_a(1, 1, True)
    state = stage_b(0, 0, max0, state)

    def trip(p, carried):
        max1, state = carried
        n = 2 * p
        max0 = stage_a(n, 0, False)
        state = stage_b(n - 1, 1, max1, state)
        max1 = stage_a(n + 1, 1, False)
        state = stage_b(n, 0, max0, state)
        return max1, state

    max1, state = lax.fori_loop(1, qi + 1, trip, (max1, state))
    state = stage_b(2 * qi + 1, 1, max1, state)
    outs = [acc[0:HEAD_DIM, :] / acc[HEAD_DIM:HEAD_DIM + 1, :] for _, acc in state]
    o_ref[0] = jnp.concatenate(outs, axis=0).T.astype(o_ref.dtype)


def _fox_scratch():
    return [pltpu.VMEM((2, 2, ATT_BLOCK, QRY_BLOCK), F32)]


def _post_attn_kernel(x_ref, ysb_ref, yfx_ref, sgsb_ref, sgfx_ref, wbsb_ref, wbfx_ref, wout_ref,
                      gpost_ref, gffn_ref, x1_ref, h2_ref):
    z_sb = _dot(ysb_ref[0], wbsb_ref[...])
    z_fx = _dot(yfx_ref[0], wbfx_ref[...])
    mixed = sgsb_ref[0].astype(F32) * z_sb + sgfx_ref[0].astype(F32) * z_fx
    x1 = x_ref[0] + _rms(_dot(mixed.astype(BF16), wout_ref[...]), gpost_ref[...])
    x1_ref[0] = x1
    h2_ref[0] = _rms(x1, gffn_ref[...]).astype(BF16)


def _post_attn(x, y_sb, y_fx, sg_sb, sg_fx, w_bsb, w_bfx, w_out, g_post, g_ffn, tm):
    b, s, d = x.shape
    tok = lambda w: pl.BlockSpec((1, tm, w), lambda bi, i: (bi, i, 0))
    return pl.pallas_call(
        _post_attn_kernel,
        grid=(b, s // tm),
        in_specs=[tok(d), tok(WIDTH), tok(WIDTH), tok(d), tok(d),
                  _const_spec((WIDTH, d)), _const_spec((WIDTH, d)), _const_spec((d, d)),
                  _const_spec((1, d)), _const_spec((1, d))],
        out_specs=(tok(d), tok(d)),
        out_shape=(jax.ShapeDtypeStruct((b, s, d), F32), jax.ShapeDtypeStruct((b, s, d), BF16)),
        compiler_params=pltpu.CompilerParams(
            dimension_semantics=("arbitrary", "arbitrary"), vmem_limit_bytes=VMEM_LIMIT),
        name="post_attn",
    )(x, y_sb, y_fx, sg_sb, sg_fx, w_bsb.astype(BF16), w_bfx.astype(BF16), w_out.astype(BF16),
      g_post.reshape(1, d).astype(F32), g_ffn.reshape(1, d).astype(F32))


def _gelu_tanh(u):
    return 0.5 * u * (1.0 + jnp.tanh(0.7978845608028654 * (u + 0.044715 * (u * u * u))))


def _ffn_ple_kernel(x1_ref, h2_ref, halo_ref, p_ref, wup_ref, cw_ref, cb_ref, wdn_ref,
                    gpost_ref, wpg_ref, wple_ref, o_ref):
    tm = h2_ref.shape[1]
    n_chunks = wup_ref.shape[0]
    fc = wdn_ref.shape[1]
    halo = jnp.where(pl.program_id(1) > 0, halo_ref[0], jnp.zeros_like(halo_ref[0]))
    h_ext = jnp.concatenate([halo, h2_ref[0]], axis=0)

    acc = jnp.zeros((tm, o_ref.shape[2]), F32)
    for c in range(n_chunks):
        u = _dot(h_ext, wup_ref[c])
        cw = cw_ref[c]
        conv = (cw[0:1, :] * u[HALO - 2:HALO - 2 + tm, :]
                + cw[1:2, :] * u[HALO - 1:HALO - 1 + tm, :]
                + cw[2:3, :] * u[HALO:HALO + tm, :]) + cb_ref[c]
        act = _gelu_tanh(conv[:, 0:fc]) * conv[:, fc:2 * fc]
        acc = acc + _dot(act.astype(BF16), wdn_ref[c])

    x2 = x1_ref[0] + _rms(acc, gpost_ref[...])
    gate = _sigmoid(_dot(x2.astype(BF16), wpg_ref[...]))
    ple = _dot(p_ref[0].astype(BF16), wple_ref[...])
    o_ref[0] = x2 + gate * ple


def _ffn_ple(x1, h2, p, w_up, conv_w, conv_b, w_down, g_post, w_ple, w_ple_gate, tm):
    b, s, d = x1.shape
    fc = FF_CHUNK
    n_chunks = D_FF // fc
    ple_dim = p.shape[-1]

    def chunked(w):
        lead = w.shape[:-1]
        w = w.reshape(lead + (2, n_chunks, fc))
        w = jnp.moveaxis(w, -2, 0)
        return w.reshape((n_chunks,) + lead + (2 * fc,))

    w_up_c = chunked(w_up).astype(BF16)
    cw_c = chunked(conv_w).astype(F32)
    cb_c = chunked(conv_b.reshape(1, 2 * D_FF)).astype(F32)
    w_dn_c = w_down.reshape(n_chunks, fc, d).astype(BF16)

    tok = lambda w: pl.BlockSpec((1, tm, w), lambda bi, i: (bi, i, 0))
    halo_spec = pl.BlockSpec(
        (1, HALO, d), lambda bi, i: (bi, jnp.maximum(i * (tm // HALO) - 1, 0), 0))
    return pl.pallas_call(
        _ffn_ple_kernel,
        grid=(b, s // tm),
        in_specs=[tok(d), tok(d), halo_spec, tok(ple_dim),
                  _const_spec(w_up_c.shape), _const_spec(cw_c.shape), _const_spec(cb_c.shape),
                  _const_spec(w_dn_c.shape), _const_spec((1, d)),
                  _const_spec((d, d)), _const_spec((ple_dim, d))],
        out_specs=tok(d),
        out_shape=jax.ShapeDtypeStruct((b, s, d), F32),
        compiler_params=pltpu.CompilerParams(
            dimension_semantics=("arbitrary", "arbitrary"), vmem_limit_bytes=VMEM_LIMIT),
        name="ffn_ple",
    )(x1, h2, h2, p, w_up_c, cw_c, cb_c, w_dn_c, g_post.reshape(1, d).astype(F32),
      w_ple_gate.astype(BF16), w_ple.astype(BF16))


def _layer(x, p, norm_attn_pre, norm_attn_post, w_in, b_forget, w_branch_sb, w_branch_fox, w_out,
           norm_ffn_pre, norm_ffn_post, w_up, conv_w, conv_b, w_down, w_ple, w_ple_gate):
    qt_sb, k_sb, vt_sb, qt_fx, k_fx, vt_fx, sg_sb, sg_fx = _in_proj(
        x, norm_attn_pre, w_in, b_forget, tm=256)
    y_sb = _attn_call(_sb_attn_kernel, "sb_attn", qt_sb, k_sb, vt_sb, HEAD_DIM, _sb_scratch())
    y_fx = _attn_call(_fox_attn_kernel, "fox_attn", qt_fx, k_fx, vt_fx, HEAD_PAD, _fox_scratch())
    x1, h2 = _post_attn(x, y_sb, y_fx, sg_sb, sg_fx, w_branch_sb, w_branch_fox, w_out,
                        norm_attn_post, norm_ffn_pre, tm=512)
    return _ffn_ple(x1, h2, p, w_up, conv_w, conv_b, w_down, norm_ffn_post, w_ple, w_ple_gate,
                    tm=512)


def kernel(x, p, norm_attn_pre, norm_attn_post, w_in, b_forget, w_branch_sb, w_branch_fox, w_out,
           norm_ffn_pre, norm_ffn_post, w_up, conv_w, conv_b, w_down, w_ple, w_ple_gate):
    depth = w_in.shape[0]
    for i in range(depth):
        x = _layer(x, p[i], norm_attn_pre[i], norm_attn_post[i], w_in[i], b_forget[i],
                   w_branch_sb[i], w_branch_fox[i], w_out[i], norm_ffn_pre[i], norm_ffn_post[i],
                   w_up[i], conv_w[i], conv_b[i], w_down[i], w_ple[i], w_ple_gate[i])
    return x
```

```python
import functools

import numpy as np
import jax
import jax.numpy as jnp
from jax import lax
from jax.experimental import pallas as pl
from jax.experimental.pallas import tpu as pltpu

HEAD_DIM = 64
N_HEADS = 8
WIDTH = N_HEADS * HEAD_DIM
HEAD_PAD = 128
D_FF = 2816
CONV_WIDTH = 3
RMS_EPS = 1e-6
SCALE = HEAD_DIM ** -0.5

ATT_BLOCK = 256
QRY_BLOCK = 512
FF_CHUNK = 256
HALO = 16
NEG_BIG = -1e30
MAX_INIT = -1e29
LOG2E = 1.4426950408889634

VMEM_LIMIT = 56 * 1024 * 1024

F32 = jnp.float32
BF16 = jnp.bfloat16


def _dot(a, b):
    return jnp.dot(a, b, preferred_element_type=F32)


def _rms(x, g):
    ms = jnp.mean(x * x, axis=-1, keepdims=True)
    return x * lax.rsqrt(ms + RMS_EPS) * g


def _log_sigmoid(z):
    return jnp.minimum(z, 0.0) - jnp.log(1.0 + jnp.exp(-jnp.abs(z)))


def _sigmoid(z):
    return 1.0 / (1.0 + jnp.exp(-z))


def _split3(c):
    c1 = c.astype(BF16)
    r = c - c1.astype(F32)
    c2 = r.astype(BF16)
    c3 = (r - c2.astype(F32)).astype(BF16)
    return c1, c2, c3


def _const_spec(shape):
    return pl.BlockSpec(shape, lambda *_: (0,) * len(shape), pipeline_mode=pl.Buffered(1))


def _in_proj_kernel(x_ref, g_ref, wsb_ref, wq_ref, wk_ref, wv_ref, wf_ref, bf_ref, wg_ref,
                    selq_ref, selk_ref, oneq_ref, onek_ref,
                    qt_sb_ref, k_sb_ref, vt_sb_ref, qt_fx_ref, k_fx_ref, vt_fx_ref,
                    sg_sb_ref, sg_fx_ref, carry_ref):
    tm = x_ref.shape[1]

    @pl.when(pl.program_id(1) == 0)
    def _():
        carry_ref[...] = jnp.zeros_like(carry_ref)

    h = _rms(x_ref[0], g_ref[...]).astype(BF16)

    logf = _log_sigmoid(_dot(h, wf_ref[...]) + bf_ref[...])
    row = lax.broadcasted_iota(jnp.int32, (tm, tm), 0)
    col = lax.broadcasted_iota(jnp.int32, (tm, tm), 1)
    tri = (col <= row).astype(BF16)
    f1, f2, f3 = _split3(logf)
    c = _dot(tri, f1) + _dot(tri, f2) + _dot(tri, f3) + carry_ref[...]
    carry_ref[...] = c[tm - 1:tm, :]
    c1, c2, c3 = _split3(c * LOG2E)

    def store_transposed_blocks(ref, t):
        for j in range(tm // ATT_BLOCK):
            ref[0, j] = t[:, j * ATT_BLOCK:(j + 1) * ATT_BLOCK].astype(BF16)

    q = _dot(h, wsb_ref[:, 0:WIDTH]) * (SCALE * LOG2E)
    qt_sb_ref[0] = q.T.astype(BF16)
    k_sb_ref[0] = _dot(h, wsb_ref[:, WIDTH:2 * WIDTH]).astype(BF16)
    store_transposed_blocks(vt_sb_ref, _dot(h, wsb_ref[:, 2 * WIDTH:3 * WIDTH]).T)

    qa = (_dot(h, wq_ref[...]) * (SCALE * LOG2E) + _dot(c1, selq_ref[0]) + _dot(c2, selq_ref[1])
          + _dot(c3, selq_ref[2]) + oneq_ref[...])
    qt_fx_ref[0] = qa.T.astype(BF16)
    ka = (_dot(h, wk_ref[...]) + _dot(c1, selk_ref[0]) + _dot(c2, selk_ref[1])
          + _dot(c3, selk_ref[2]) + onek_ref[...])
    k_fx_ref[0] = ka.astype(BF16)
    store_transposed_blocks(vt_fx_ref, _dot(h, wv_ref[...]).T)

    d = sg_sb_ref.shape[2]
    sg_sb_ref[0] = _sigmoid(_dot(h, wg_ref[:, 0:d])).astype(BF16)
    sg_fx_ref[0] = _sigmoid(_dot(h, wg_ref[:, d:2 * d])).astype(BF16)


def _in_proj(x, g_pre, w_in, b_forget, tm):
    b, s, d = x.shape
    nblk = s // ATT_BLOCK
    o = np.cumsum([0, WIDTH, WIDTH, WIDTH, WIDTH, WIDTH, WIDTH, N_HEADS, d, d])
    w_sb = w_in[:, o[0]:o[3]].astype(BF16)

    def pad_heads(w):
        w = w.reshape(d, N_HEADS, HEAD_DIM)
        w = jnp.pad(w, ((0, 0), (0, 0), (0, HEAD_PAD - HEAD_DIM)))
        return w.reshape(d, N_HEADS * HEAD_PAD).astype(BF16)

    w_q = pad_heads(w_in[:, o[3]:o[4]])
    w_k = pad_heads(w_in[:, o[4]:o[5]])
    w_v = w_in[:, o[5]:o[6]].astype(BF16)
    w_f = jnp.pad(w_in[:, o[6]:o[7]], ((0, 0), (0, 128 - N_HEADS))).astype(BF16)
    b_f = jnp.pad(b_forget.astype(F32), (0, 128 - N_HEADS)).reshape(1, 128)
    w_g = w_in[:, o[7]:o[9]].astype(BF16)

    selq = np.zeros((3, 128, N_HEADS * HEAD_PAD), np.float32)
    selk = np.zeros((3, 128, N_HEADS * HEAD_PAD), np.float32)
    oneq = np.zeros((1, N_HEADS * HEAD_PAD), np.float32)
    onek = np.zeros((1, N_HEADS * HEAD_PAD), np.float32)
    for hd in range(N_HEADS):
        base = hd * HEAD_PAD + HEAD_DIM
        for i in range(3):
            oneq[0, base + i] = 1.0
            selk[i, hd, base + i] = -1.0
            selq[i, hd, base + 3 + i] = 1.0
            onek[0, base + 3 + i] = 1.0
    selq = jnp.asarray(selq, BF16)
    selk = jnp.asarray(selk, BF16)

    nj = tm // ATT_BLOCK
    out_shape = (
        jax.ShapeDtypeStruct((b, WIDTH, s), BF16),
        jax.ShapeDtypeStruct((b, s, WIDTH), BF16),
        jax.ShapeDtypeStruct((b, nblk, WIDTH, ATT_BLOCK), BF16),
        jax.ShapeDtypeStruct((b, N_HEADS * HEAD_PAD, s), BF16),
        jax.ShapeDtypeStruct((b, s, N_HEADS * HEAD_PAD), BF16),
        jax.ShapeDtypeStruct((b, nblk, WIDTH, ATT_BLOCK), BF16),
        jax.ShapeDtypeStruct((b, s, d), BF16),
        jax.ShapeDtypeStruct((b, s, d), BF16),
    )
    out_specs = (
        pl.BlockSpec((1, WIDTH, tm), lambda bi, i: (bi, 0, i)),
        pl.BlockSpec((1, tm, WIDTH), lambda bi, i: (bi, i, 0)),
        pl.BlockSpec((1, nj, WIDTH, ATT_BLOCK), lambda bi, i: (bi, i, 0, 0)),
        pl.BlockSpec((1, N_HEADS * HEAD_PAD, tm), lambda bi, i: (bi, 0, i)),
        pl.BlockSpec((1, tm, N_HEADS * HEAD_PAD), lambda bi, i: (bi, i, 0)),
        pl.BlockSpec((1, nj, WIDTH, ATT_BLOCK), lambda bi, i: (bi, i, 0, 0)),
        pl.BlockSpec((1, tm, d), lambda bi, i: (bi, i, 0)),
        pl.BlockSpec((1, tm, d), lambda bi, i: (bi, i, 0)),
    )
    in_specs = [
        pl.BlockSpec((1, tm, d), lambda bi, i: (bi, i, 0)),
        _const_spec((1, d)),
        _const_spec(w_sb.shape), _const_spec(w_q.shape), _const_spec(w_k.shape),
        _const_spec(w_v.shape), _const_spec(w_f.shape), _const_spec(b_f.shape),
        _const_spec(w_g.shape), _const_spec(selq.shape), _const_spec(selk.shape),
        _const_spec(oneq.shape), _const_spec(onek.shape),
    ]
    return pl.pallas_call(
        _in_proj_kernel,
        grid=(b, s // tm),
        in_specs=in_specs,
        out_specs=out_specs,
        out_shape=out_shape,
        scratch_shapes=[pltpu.VMEM((1, 128), F32)],
        compiler_params=pltpu.CompilerParams(
            dimension_semantics=("arbitrary", "arbitrary"), vmem_limit_bytes=VMEM_LIMIT),
        name="in_proj",
    )(x, g_pre.reshape(1, d).astype(F32), w_sb, w_q, w_k, w_v, w_f, b_f, w_g,
      selq, selk, jnp.asarray(oneq), jnp.asarray(onek))


def _diag_offsets():
    tk, tq = ATT_BLOCK, QRY_BLOCK
    offsets = (lax.broadcasted_iota(jnp.int32, (tk, tq), 0)
               - lax.broadcasted_iota(jnp.int32, (tk, tq), 1))
    return offsets, lambda n: (n + 1 - tq // tk) * tk


def _neg_abs(z):
    bits = lax.bitcast_convert_type(z, jnp.uint32) | jnp.uint32(0x80000000)
    return lax.bitcast_convert_type(bits, F32)


def _sb_attn_kernel(qt_ref, k_ref, vt_ref, o_ref, lr_ref, lb_ref, t_ref):
    tk, tq = ATT_BLOCK, QRY_BLOCK
    per_q = tq // tk
    assert per_q == 2, "the two scratch slots assume an even number of key blocks per query block"
    qi = pl.program_id(2)
    last = (qi + 1) * per_q - 1
    qt = qt_ref[0]
    head_row = lax.broadcasted_iota(jnp.int32, qt.shape, 0)
    zero = jnp.zeros_like(qt)
    qth = (jnp.where(head_row < HEAD_DIM, qt, zero),
           jnp.where(head_row >= HEAD_DIM, qt, zero))
    offsets, shift = _diag_offsets()
    row = lax.broadcasted_iota(jnp.int32, (tk, tk), 0)
    col = lax.broadcasted_iota(jnp.int32, (tk, tk), 1)
    suffix = (col > row).astype(BF16)

    def stage_a(n, slot, diag):
        kj = last - n
        kblk = k_ref[0, pl.ds(pl.multiple_of(kj * tk, tk), tk), :]
        if diag:
            causal = offsets < shift(n)
        first_rows = []
        for hh in range(2):
            z = _dot(kblk, qth[hh])
            log_beta = jnp.minimum(z, 0.0) - jnp.log2(1.0 + jnp.exp2(_neg_abs(z)))
            log_rest = log_beta - z
            if diag:
                log_rest = jnp.where(causal, log_rest, 0.0)
                log_beta = jnp.where(causal, log_beta, NEG_BIG)
            lr_ref[slot, hh] = log_rest.astype(BF16)
            lb_ref[slot, hh] = log_beta
            first_rows.append(log_rest[0:1, :])
        return tuple(first_rows)

    def stage_b(slot, first_rows):
        col_sums = []
        for hh in range(2):
            within = _dot(suffix, lr_ref[slot, hh])
            t_ref[slot, hh] = lb_ref[slot, hh] + within
            col_sums.append(within[0:1, :] + first_rows[hh])
        return tuple(col_sums)

    def weights(slot, state):
        return tuple(jnp.exp2(t_ref[slot, hh] + state[hh][0]).astype(BF16) for hh in range(2))

    def accumulate(n, w, col_sums, state):
        kj = last - n
        new_state = []
        for hh in range(2):
            carry, acc = state[hh]
            vt = vt_ref[0, kj, hh * HEAD_DIM:(hh + 1) * HEAD_DIM, :]
            new_state.append((carry + col_sums[hh], acc + _dot(vt, w[hh])))
        return tuple(new_state)

    def stage_c(n, slot, col_sums, state):
        return accumulate(n, weights(slot, state), col_sums, state)

    state = tuple((jnp.zeros((1, tq), F32), jnp.zeros((HEAD_DIM, tq), F32)) for _ in range(2))
    rows0 = stage_a(0, 0, True)
    rows1 = stage_a(1, 1, True)
    sums0 = stage_b(0, rows0)

    def trip(p, carried):
        rows1, sums0, state = carried
        n = 2 * p
        rows0 = stage_a(n, 0, False)
        state = stage_c(n - 2, 0, sums0, state)
        sums1 = stage_b(1, rows1)
        rows1 = stage_a(n + 1, 1, False)
        state = stage_c(n - 1, 1, sums1, state)
        sums0 = stage_b(0, rows0)
        return rows1, sums0, state

    rows1, sums0, state = lax.fori_loop(1, qi + 1, trip, (rows1, sums0, state))
    sums1 = stage_b(1, rows1)
    state = stage_c(last - 1, 0, sums0, state)
    state = stage_c(last, 1, sums1, state)
    o_ref[0] = jnp.concatenate([st[1] for st in state], axis=0).T.astype(o_ref.dtype)


def _sb_scratch():
    tk, tq = ATT_BLOCK, QRY_BLOCK
    return [pltpu.VMEM((2, 2, tk, tq), BF16),
            pltpu.VMEM((2, 2, tk, tq), F32), pltpu.VMEM((2, 2, tk, tq), F32)]


def _attn_call(kernel_fn, name, qt, k, vt, head_rows, scratch):
    b, _, s = qt.shape
    tk, tq = ATT_BLOCK, QRY_BLOCK
    return pl.pallas_call(
        kernel_fn,
        grid=(b, N_HEADS // 2, s // tq),
        in_specs=[
            pl.BlockSpec((1, 2 * head_rows, tq), lambda bi, hp, qi: (bi, hp, qi)),
            pl.BlockSpec((1, s, 2 * head_rows), lambda bi, hp, qi: (bi, 0, hp)),
            pl.BlockSpec((1, s // tk, 2 * HEAD_DIM, tk), lambda bi, hp, qi: (bi, 0, hp, 0)),
        ],
        out_specs=pl.BlockSpec((1, tq, 2 * HEAD_DIM), lambda bi, hp, qi: (bi, qi, hp)),
        out_shape=jax.ShapeDtypeStruct((b, s, WIDTH), BF16),
        scratch_shapes=scratch,
        compiler_params=pltpu.CompilerParams(
            dimension_semantics=("arbitrary", "arbitrary", "arbitrary"),
            vmem_limit_bytes=VMEM_LIMIT),
        name=name,
    )(qt, k, vt)


def _fox_attn_kernel(qt_ref, k_ref, vt_ref, o_ref, s_ref):
    tk, tq = ATT_BLOCK, QRY_BLOCK
    per_q = tq // tk
    assert per_q == 2, "the two scratch slots assume an even number of key blocks per query block"
    qi = pl.program_id(2)
    last = (qi + 1) * per_q - 1
    offsets, shift = _diag_offsets()
    ones = jnp.ones((HALO, tk), BF16)
    qth = tuple(qt_ref[0, hh * HEAD_PAD:(hh + 1) * HEAD_PAD, :] for hh in range(2))

    def stage_a(n, slot, diag):
        kj = last - n
        rows = pl.ds(pl.multiple_of(kj * tk, tk), tk)
        if diag:
            causal = offsets <= shift(n)
        maxes = []
        for hh in range(2):
            kblk = k_ref[0, rows, hh * HEAD_PAD:(hh + 1) * HEAD_PAD]
            logits = _dot(kblk, qth[hh])
            if diag:
                logits = jnp.where(causal, logits, NEG_BIG)
            s_ref[slot, hh] = logits
            maxes.append(jnp.max(logits, axis=0, keepdims=True))
        return tuple(maxes)

    def stage_b(n, slot, maxes, state):
        kj = last - n
        new_state = []
        for hh in range(2):
            m, acc = state[hh]
            m_new = jnp.maximum(m, maxes[hh])
            p = jnp.exp2(s_ref[slot, hh] - m_new)
            vt = vt_ref[0, kj, hh * HEAD_DIM:(hh + 1) * HEAD_DIM, :]
            vt_aug = jnp.concatenate([vt, ones], axis=0)
            acc = acc * jnp.exp2(m - m_new) + _dot(vt_aug, p.astype(BF16))
            new_state.append((m_new, acc))
        return tuple(new_state)

    state = tuple((jnp.full((1, tq), MAX_INIT, F32), jnp.zeros((HEAD_DIM + HALO, tq), F32))
                  for _ in range(2))
    max0 = stage_a(0, 0, True)
    max1 = stage_a(1, 1, True)
    state = stage_b(0, 0, max0, state)

    def trip(p, carried):
        max1, state = carried
        n = 2 * p
        max0 = stage_a(n, 0, False)
        state = stage_b(n - 1, 1, max1, state)
        max1 = stage_a(n + 1, 1, False)
        state = stage_b(n, 0, max0, state)
        return max1, state

    max1, state = lax.fori_loop(1, qi + 1, trip, (max1, state))
    state = stage_b(2 * qi + 1, 1, max1, state)
    outs = [acc[0:HEAD_DIM, :] / acc[HEAD_DIM:HEAD_DIM + 1, :] for _, acc in state]
    o_ref[0] = jnp.concatenate(outs, axis=0).T.astype(o_ref.dtype)


def _fox_scratch():
    return [pltpu.VMEM((2, 2, ATT_BLOCK, QRY_BLOCK), F32)]


def _post_attn_kernel(x_ref, ysb_ref, yfx_ref, sgsb_ref, sgfx_ref, wbsb_ref, wbfx_ref, wout_ref,
                      gpost_ref, gffn_ref, x1_ref, h2_ref):
    z_sb = _dot(ysb_ref[0], wbsb_ref[...])
    z_fx = _dot(yfx_ref[0], wbfx_ref[...])
    mixed = sgsb_ref[0].astype(F32) * z_sb + sgfx_ref[0].astype(F32) * z_fx
    x1 = x_ref[0] + _rms(_dot(mixed.astype(BF16), wout_ref[...]), gpost_ref[...])
    x1_ref[0] = x1
    h2_ref[0] = _rms(x1, gffn_ref[...]).astype(BF16)


def _post_attn(x, y_sb, y_fx, sg_sb, sg_fx, w_bsb, w_bfx, w_out, g_post, g_ffn, tm):
    b, s, d = x.shape
    tok = lambda w: pl.BlockSpec((1, tm, w), lambda bi, i: (bi, i, 0))
    return pl.pallas_call(
        _post_attn_kernel,
        grid=(b, s // tm),
        in_specs=[tok(d), tok(WIDTH), tok(WIDTH), tok(d), tok(d),
                  _const_spec((WIDTH, d)), _const_spec((WIDTH, d)), _const_spec((d, d)),
                  _const_spec((1, d)), _const_spec((1, d))],
        out_specs=(tok(d), tok(d)),
        out_shape=(jax.ShapeDtypeStruct((b, s, d), F32), jax.ShapeDtypeStruct((b, s, d), BF16)),
        compiler_params=pltpu.CompilerParams(
            dimension_semantics=("arbitrary", "arbitrary"), vmem_limit_bytes=VMEM_LIMIT),
        name="post_attn",
    )(x, y_sb, y_fx, sg_sb, sg_fx, w_bsb.astype(BF16), w_bfx.astype(BF16), w_out.astype(BF16),
      g_post.reshape(1, d).astype(F32), g_ffn.reshape(1, d).astype(F32))


def _gelu_tanh(u):
    return 0.5 * u * (1.0 + jnp.tanh(0.7978845608028654 * (u + 0.044715 * (u * u * u))))


def _ffn_ple_kernel(x1_ref, h2_ref, halo_ref, p_ref, wup_ref, cw_ref, cb_ref, wdn_ref,
                    gpost_ref, wpg_ref, wple_ref, o_ref, u0_ref, u1_ref):
    tm = h2_ref.shape[1]
    fc = FF_CHUNK
    n_chunks = D_FF // fc
    halo = jnp.where(pl.program_id(1) > 0, halo_ref[0], jnp.zeros_like(halo_ref[0]))
    h_ext = jnp.concatenate([halo, h2_ref[0]], axis=0)
    u_refs = (u0_ref, u1_ref)

    def up_project(c):
        for half in range(2):
            cols = slice(half * D_FF + c * fc, half * D_FF + (c + 1) * fc)
            u_refs[c % 2][half] = _dot(h_ext, wup_ref[:, cols])

    def conv(c, half):
        cols = slice(half * D_FF + c * fc, half * D_FF + (c + 1) * fc)
        u_ref = u_refs[c % 2]
        out = cb_ref[:, cols]
        for tap in range(CONV_WIDTH):
            shift = CONV_WIDTH - 1 - tap
            out = out + cw_ref[tap:tap + 1, cols] * u_ref[half, HALO - shift:HALO - shift + tm, :]
        return out

    acc = jnp.zeros((tm, o_ref.shape[2]), F32)
    up_project(0)
    for c in range(n_chunks):
        if c + 1 < n_chunks:
            up_project(c + 1)
        act = _gelu_tanh(conv(c, 0)) * conv(c, 1)
        acc = acc + _dot(act.astype(BF16), wdn_ref[c * fc:(c + 1) * fc, :])

    x2 = x1_ref[0] + _rms(acc, gpost_ref[...])
    gate = _sigmoid(_dot(x2.astype(BF16), wpg_ref[...]))
    ple = _dot(p_ref[0].astype(BF16), wple_ref[...])
    o_ref[0] = x2 + gate * ple


def _ffn_ple(x1, h2, p, w_up, conv_w, conv_b, w_down, g_post, w_ple, w_ple_gate, tm):
    b, s, d = x1.shape
    assert D_FF % FF_CHUNK == 0 and w_up.shape == (d, 2 * D_FF)
    ple_dim = p.shape[-1]
    tok = lambda w: pl.BlockSpec((1, tm, w), lambda bi, i: (bi, i, 0))
    halo_spec = pl.BlockSpec(
        (1, HALO, d), lambda bi, i: (bi, jnp.maximum(i * (tm // HALO) - 1, 0), 0))
    u_scratch = pltpu.VMEM((2, HALO + tm, FF_CHUNK), F32)
    return pl.pallas_call(
        _ffn_ple_kernel,
        grid=(b, s // tm),
        in_specs=[tok(d), tok(d), halo_spec, tok(ple_dim),
                  _const_spec((d, 2 * D_FF)), _const_spec((CONV_WIDTH, 2 * D_FF)),
                  _const_spec((1, 2 * D_FF)), _const_spec((D_FF, d)), _const_spec((1, d)),
                  _const_spec((d, d)), _const_spec((ple_dim, d))],
        out_specs=tok(d),
        out_shape=jax.ShapeDtypeStruct((b, s, d), F32),
        scratch_shapes=[u_scratch, u_scratch],
        compiler_params=pltpu.CompilerParams(
            dimension_semantics=("arbitrary", "arbitrary"), vmem_limit_bytes=VMEM_LIMIT),
        name="ffn_ple",
    )(x1, h2, h2, p, w_up.astype(BF16), conv_w.astype(F32), conv_b.reshape(1, 2 * D_FF).astype(F32),
      w_down.astype(BF16), g_post.reshape(1, d).astype(F32),
      w_ple_gate.astype(BF16), w_ple.astype(BF16))


def _layer(x, p, norm_attn_pre, norm_attn_post, w_in, b_forget, w_branch_sb, w_branch_fox, w_out,
           norm_ffn_pre, norm_ffn_post, w_up, conv_w, conv_b, w_down, w_ple, w_ple_gate):
    qt_sb, k_sb, vt_sb, qt_fx, k_fx, vt_fx, sg_sb, sg_fx = _in_proj(
        x, norm_attn_pre, w_in, b_forget, tm=256)
    y_sb = _attn_call(_sb_attn_kernel, "sb_attn", qt_sb, k_sb, vt_sb, HEAD_DIM, _sb_scratch())
    y_fx = _attn_call(_fox_attn_kernel, "fox_attn", qt_fx, k_fx, vt_fx, HEAD_PAD, _fox_scratch())
    x1, h2 = _post_attn(x, y_sb, y_fx, sg_sb, sg_fx, w_branch_sb, w_branch_fox, w_out,
                        norm_attn_post, norm_ffn_pre, tm=512)
    return _ffn_ple(x1, h2, p, w_up, conv_w, conv_b, w_down, norm_ffn_post, w_ple, w_ple_gate,
                    tm=512)


def kernel(x, p, norm_attn_pre, norm_attn_post, w_in, b_forget, w_branch_sb, w_branch_fox, w_out,
           norm_ffn_pre, norm_ffn_post, w_up, conv_w, conv_b, w_down, w_ple, w_ple_gate):
    depth = w_in.shape[0]
    for i in range(depth):
        x = _layer(x, p[i], norm_attn_pre[i], norm_attn_post[i], w_in[i], b_forget[i],
                   w_branch_sb[i], w_branch_fox[i], w_out[i], norm_ffn_pre[i], norm_ffn_post[i],
                   w_up[i], conv_w[i], conv_b[i], w_down[i], w_ple[i], w_ple_gate[i])
    return x
```

```python
import functools

import numpy as np
import jax
import jax.numpy as jnp
from jax import lax
from jax.experimental import pallas as pl
from jax.experimental.pallas import tpu as pltpu

HEAD_DIM = 64
N_HEADS = 8
WIDTH = N_HEADS * HEAD_DIM
AUG_PIECES = 3
AUG_ROWS = 8
D_FF = 2816
CONV_WIDTH = 3
RMS_EPS = 1e-6
SCALE = HEAD_DIM ** -0.5

ATT_BLOCK = 256
QRY_BLOCK = 512
FF_CHUNK = 256
HALO = 16
NEG_BIG = -1e30
MAX_INIT = -1e29
LOG2E = 1.4426950408889634

VMEM_LIMIT = 56 * 1024 * 1024

F32 = jnp.float32
BF16 = jnp.bfloat16


def _dot(a, b):
    return jnp.dot(a, b, preferred_element_type=F32)


def _rms(x, g):
    ms = jnp.mean(x * x, axis=-1, keepdims=True)
    return x * lax.rsqrt(ms + RMS_EPS) * g


def _log_sigmoid(z):
    return jnp.minimum(z, 0.0) - jnp.log(1.0 + jnp.exp(-jnp.abs(z)))


def _sigmoid(z):
    return 1.0 / (1.0 + jnp.exp(-z))


def _split3(c):
    c1 = c.astype(BF16)
    r = c - c1.astype(F32)
    c2 = r.astype(BF16)
    c3 = (r - c2.astype(F32)).astype(BF16)
    return c1, c2, c3


def _const_spec(shape):
    return pl.BlockSpec(shape, lambda *_: (0,) * len(shape), pipeline_mode=pl.Buffered(1))


def _in_proj_kernel(x_ref, g_ref, wqkv_ref, wf_ref, bf_ref, wg_ref,
                    selq_ref, selk_ref, oneq_ref, onek_ref,
                    qt_sb_ref, k_sb_ref, vt_sb_ref, qt_fx_ref, qa_fx_ref, k_fx_ref, ka_fx_ref,
                    vt_fx_ref, sg_sb_ref, sg_fx_ref, carry_ref):
    tm = x_ref.shape[1]

    @pl.when(pl.program_id(1) == 0)
    def _():
        carry_ref[...] = jnp.zeros_like(carry_ref)

    h = _rms(x_ref[0], g_ref[...]).astype(BF16)

    logf = _log_sigmoid(_dot(h, wf_ref[...]) + bf_ref[...])
    row = lax.broadcasted_iota(jnp.int32, (tm, tm), 0)
    col = lax.broadcasted_iota(jnp.int32, (tm, tm), 1)
    tri = (col <= row).astype(BF16)
    f1, f2, f3 = _split3(logf)
    c = _dot(tri, f1) + _dot(tri, f2) + _dot(tri, f3) + carry_ref[...]
    carry_ref[...] = c[tm - 1:tm, :]
    c1, c2, c3 = _split3(c * LOG2E)
    lane = lax.broadcasted_iota(jnp.int32, c.shape, 1)
    pieces = jnp.where(lane < N_HEADS, c1.astype(F32),
                       jnp.where(lane < 2 * N_HEADS, c2.astype(F32), c3.astype(F32))).astype(BF16)

    def project(j):
        return _dot(h, wqkv_ref[:, j * WIDTH:(j + 1) * WIDTH])

    def store_transposed_blocks(ref, t):
        for j in range(tm // ATT_BLOCK):
            ref[0, j] = t[:, j * ATT_BLOCK:(j + 1) * ATT_BLOCK].astype(BF16)

    qt_sb_ref[0] = (project(0) * (SCALE * LOG2E)).T.astype(BF16)
    k_sb_ref[0] = project(1).astype(BF16)
    store_transposed_blocks(vt_sb_ref, project(2).T)

    qt_fx_ref[0] = (project(3) * (SCALE * LOG2E)).T.astype(BF16)
    k_fx_ref[0] = project(4).astype(BF16)
    store_transposed_blocks(vt_fx_ref, project(5).T)
    qa_fx_ref[0] = (_dot(pieces, selq_ref[...]) + oneq_ref[...]).T.astype(BF16)
    ka_fx_ref[0] = (_dot(pieces, selk_ref[...]) + onek_ref[...]).astype(BF16)

    d = sg_sb_ref.shape[2]
    sg_sb_ref[0] = _sigmoid(_dot(h, wg_ref[:, 0:d])).astype(BF16)
    sg_fx_ref[0] = _sigmoid(_dot(h, wg_ref[:, d:2 * d])).astype(BF16)


def _in_proj(x, g_pre, w_in, b_forget, tm):
    b, s, d = x.shape
    nblk = s // ATT_BLOCK
    n_qkv = 6 * WIDTH
    w_qkv = w_in[:, 0:n_qkv].astype(BF16)
    w_f = w_in[:, n_qkv:n_qkv + N_HEADS]
    pad = 128 - AUG_PIECES * N_HEADS
    w_f = jnp.pad(jnp.tile(w_f, (1, AUG_PIECES)), ((0, 0), (0, pad))).astype(BF16)
    b_f = jnp.pad(jnp.tile(b_forget.astype(F32), AUG_PIECES), (0, pad)).reshape(1, 128)
    w_g = w_in[:, n_qkv + N_HEADS:].astype(BF16)

    selq = np.zeros((128, WIDTH), np.float32)
    selk = np.zeros((128, WIDTH), np.float32)
    oneq = np.zeros((1, WIDTH), np.float32)
    onek = np.zeros((1, WIDTH), np.float32)
    for hd in range(N_HEADS):
        base = (hd // 2) * 128 + (hd % 2) * AUG_ROWS
        for i in range(AUG_PIECES):
            selk[i * N_HEADS + hd, base + i] = -1.0
            onek[0, base + AUG_PIECES + i] = 1.0
            oneq[0, base + i] = 1.0
            selq[i * N_HEADS + hd, base + AUG_PIECES + i] = 1.0
    selq = jnp.asarray(selq, BF16)
    selk = jnp.asarray(selk, BF16)

    nj = tm // ATT_BLOCK
    rows_t = jax.ShapeDtypeStruct((b, WIDTH, s), BF16)
    rows_n = jax.ShapeDtypeStruct((b, s, WIDTH), BF16)
    v_blocks = jax.ShapeDtypeStruct((b, nblk, WIDTH, ATT_BLOCK), BF16)
    gate = jax.ShapeDtypeStruct((b, s, d), BF16)
    spec_t = pl.BlockSpec((1, WIDTH, tm), lambda bi, i: (bi, 0, i))
    spec_n = pl.BlockSpec((1, tm, WIDTH), lambda bi, i: (bi, i, 0))
    spec_v = pl.BlockSpec((1, nj, WIDTH, ATT_BLOCK), lambda bi, i: (bi, i, 0, 0))
    spec_g = pl.BlockSpec((1, tm, d), lambda bi, i: (bi, i, 0))
    out_shape = (rows_t, rows_n, v_blocks, rows_t, rows_t, rows_n, rows_n, v_blocks, gate, gate)
    out_specs = (spec_t, spec_n, spec_v, spec_t, spec_t, spec_n, spec_n, spec_v, spec_g, spec_g)
    in_specs = [
        pl.BlockSpec((1, tm, d), lambda bi, i: (bi, i, 0)),
        _const_spec((1, d)),
        _const_spec(w_qkv.shape), _const_spec(w_f.shape), _const_spec(b_f.shape),
        _const_spec(w_g.shape), _const_spec(selq.shape), _const_spec(selk.shape),
        _const_spec(oneq.shape), _const_spec(onek.shape),
    ]
    return pl.pallas_call(
        _in_proj_kernel,
        grid=(b, s // tm),
        in_specs=in_specs,
        out_specs=out_specs,
        out_shape=out_shape,
        scratch_shapes=[pltpu.VMEM((1, 128), F32)],
        compiler_params=pltpu.CompilerParams(
            dimension_semantics=("arbitrary", "arbitrary"), vmem_limit_bytes=VMEM_LIMIT),
        name="in_proj",
    )(x, g_pre.reshape(1, d).astype(F32), w_qkv, w_f, b_f, w_g,
      selq, selk, jnp.asarray(oneq), jnp.asarray(onek))


def _diag_offsets():
    tk, tq = ATT_BLOCK, QRY_BLOCK
    offsets = (lax.broadcasted_iota(jnp.int32, (tk, tq), 0)
               - lax.broadcasted_iota(jnp.int32, (tk, tq), 1))
    return offsets, lambda n: (n + 1 - tq // tk) * tk


def _paired_loop(first, stop, trip, carried):
    pairs = lax.shift_right_logical(stop - first, 1)

    def two_trips(i, c):
        p = first + 2 * i
        return trip(p + 1, trip(p, c))

    carried = lax.fori_loop(0, pairs, two_trips, carried)
    return lax.fori_loop(first + 2 * pairs, stop, trip, carried)


def _neg_abs(z):
    bits = lax.bitcast_convert_type(z, jnp.uint32) | jnp.uint32(0x80000000)
    return lax.bitcast_convert_type(bits, F32)


def _sb_attn_kernel(qt_ref, k_ref, vt_ref, o_ref, lr_ref, lb_ref, t_ref):
    tk, tq = ATT_BLOCK, QRY_BLOCK
    per_q = tq // tk
    assert per_q == 2, "the two scratch slots assume an even number of key blocks per query block"
    qi = pl.program_id(2)
    last = (qi + 1) * per_q - 1
    qt = qt_ref[0]
    head_row = lax.broadcasted_iota(jnp.int32, qt.shape, 0)
    zero = jnp.zeros_like(qt)
    qth = (jnp.where(head_row < HEAD_DIM, qt, zero),
           jnp.where(head_row >= HEAD_DIM, qt, zero))
    offsets, shift = _diag_offsets()
    row = lax.broadcasted_iota(jnp.int32, (tk, tk), 0)
    col = lax.broadcasted_iota(jnp.int32, (tk, tk), 1)
    suffix = (col > row).astype(BF16)

    def stage_a(n, slot, diag):
        kj = last - n
        kblk = k_ref[0, pl.ds(pl.multiple_of(kj * tk, tk), tk), :]
        if diag:
            causal = offsets < shift(n)
        first_rows = []
        for hh in range(2):
            z = _dot(kblk, qth[hh])
            log_beta = jnp.minimum(z, 0.0) - jnp.log2(1.0 + jnp.exp2(_neg_abs(z)))
            log_rest = log_beta - z
            if diag:
                log_rest = jnp.where(causal, log_rest, 0.0)
                log_beta = jnp.where(causal, log_beta, NEG_BIG)
            lr_ref[slot, hh] = log_rest.astype(BF16)
            lb_ref[slot, hh] = log_beta
            first_rows.append(log_rest[0:1, :])
        return tuple(first_rows)

    def stage_b(slot, first_rows):
        col_sums = []
        for hh in range(2):
            within = _dot(suffix, lr_ref[slot, hh])
            t_ref[slot, hh] = lb_ref[slot, hh] + within
            col_sums.append(within[0:1, :] + first_rows[hh])
        return tuple(col_sums)

    def weights(slot, state):
        return tuple(jnp.exp2(t_ref[slot, hh] + state[hh][0]).astype(BF16) for hh in range(2))

    def accumulate(n, w, col_sums, state):
        kj = last - n
        new_state = []
        for hh in range(2):
            carry, acc = state[hh]
            vt = vt_ref[0, kj, hh * HEAD_DIM:(hh + 1) * HEAD_DIM, :]
            new_state.append((carry + col_sums[hh], acc + _dot(vt, w[hh])))
        return tuple(new_state)

    def stage_c(n, slot, col_sums, state):
        return accumulate(n, weights(slot, state), col_sums, state)

    state = tuple((jnp.zeros((1, tq), F32), jnp.zeros((HEAD_DIM, tq), F32)) for _ in range(2))
    rows0 = stage_a(0, 0, True)
    rows1 = stage_a(1, 1, True)
    sums0 = stage_b(0, rows0)

    def trip(p, carried):
        rows1, sums0, state = carried
        n = 2 * p
        rows0 = stage_a(n, 0, False)
        state = stage_c(n - 2, 0, sums0, state)
        sums1 = stage_b(1, rows1)
        rows1 = stage_a(n + 1, 1, False)
        state = stage_c(n - 1, 1, sums1, state)
        sums0 = stage_b(0, rows0)
        return rows1, sums0, state

    rows1, sums0, state = _paired_loop(1, qi + 1, trip, (rows1, sums0, state))
    sums1 = stage_b(1, rows1)
    state = stage_c(last - 1, 0, sums0, state)
    state = stage_c(last, 1, sums1, state)
    o_ref[0] = jnp.concatenate([st[1] for st in state], axis=0).T.astype(o_ref.dtype)


def _sb_scratch():
    tk, tq = ATT_BLOCK, QRY_BLOCK
    return [pltpu.VMEM((2, 2, tk, tq), BF16),
            pltpu.VMEM((2, 2, tk, tq), F32), pltpu.VMEM((2, 2, tk, tq), F32)]


def _attn_call(kernel_fn, name, qts, ks, vt, scratch):
    b, _, s = qts[0].shape
    tk, tq = ATT_BLOCK, QRY_BLOCK
    pair = 2 * HEAD_DIM
    qt_spec = pl.BlockSpec((1, pair, tq), lambda bi, hp, qi: (bi, hp, qi))
    k_spec = pl.BlockSpec((1, s, pair), lambda bi, hp, qi: (bi, 0, hp))
    vt_spec = pl.BlockSpec((1, s // tk, pair, tk), lambda bi, hp, qi: (bi, 0, hp, 0))
    return pl.pallas_call(
        kernel_fn,
        grid=(b, N_HEADS // 2, s // tq),
        in_specs=[qt_spec] * len(qts) + [k_spec] * len(ks) + [vt_spec],
        out_specs=pl.BlockSpec((1, tq, pair), lambda bi, hp, qi: (bi, qi, hp)),
        out_shape=jax.ShapeDtypeStruct((b, s, WIDTH), BF16),
        scratch_shapes=scratch,
        compiler_params=pltpu.CompilerParams(
            dimension_semantics=("arbitrary", "arbitrary", "arbitrary"),
            vmem_limit_bytes=VMEM_LIMIT),
        name=name,
    )(*qts, *ks, vt)


def _fox_attn_kernel(qt_ref, qa_ref, k_ref, ka_ref, vt_ref, o_ref, s_ref):
    tk, tq = ATT_BLOCK, QRY_BLOCK
    per_q = tq // tk
    assert per_q == 2, "the two scratch slots assume an even number of key blocks per query block"
    qi = pl.program_id(2)
    last = (qi + 1) * per_q - 1
    offsets, shift = _diag_offsets()
    ones = jnp.ones((HALO, tk), BF16)
    qt, qa = qt_ref[0], qa_ref[0]
    row = lax.broadcasted_iota(jnp.int32, qt.shape, 0)
    zero = jnp.zeros_like(qt)
    qth = tuple(
        jnp.concatenate([jnp.where((row >= hh * HEAD_DIM) & (row < (hh + 1) * HEAD_DIM), qt, zero),
                         jnp.where((row >= hh * AUG_ROWS) & (row < (hh + 1) * AUG_ROWS), qa, zero)],
                        axis=0)
        for hh in range(2))

    def stage_a(n, slot, diag):
        kj = last - n
        rows = pl.ds(pl.multiple_of(kj * tk, tk), tk)
        if diag:
            causal = offsets <= shift(n)
        kblk = jnp.concatenate([k_ref[0, rows, :], ka_ref[0, rows, :]], axis=1)
        maxes = []
        for hh in range(2):
            logits = _dot(kblk, qth[hh])
            if diag:
                logits = jnp.where(causal, logits, NEG_BIG)
            s_ref[slot, hh] = logits
            maxes.append(jnp.max(logits, axis=0, keepdims=True))
        return tuple(maxes)

    def stage_b(n, slot, maxes, state):
        kj = last - n
        new_state = []
        for hh in range(2):
            m, acc = state[hh]
            m_new = jnp.maximum(m, maxes[hh])
            p = jnp.exp2(s_ref[slot, hh] - m_new)
            vt = vt_ref[0, kj, hh * HEAD_DIM:(hh + 1) * HEAD_DIM, :]
            vt_aug = jnp.concatenate([vt, ones], axis=0)
            acc = acc * jnp.exp2(m - m_new) + _dot(vt_aug, p.astype(BF16))
            new_state.append((m_new, acc))
        return tuple(new_state)

    state = tuple((jnp.full((1, tq), MAX_INIT, F32), jnp.zeros((HEAD_DIM + HALO, tq), F32))
                  for _ in range(2))
    max0 = stage_a(0, 0, True)
    max1 = stage_a(1, 1, True)
    state = stage_b(0, 0, max0, state)

    def trip(p, carried):
        max1, state = carried
        n = 2 * p
        max0 = stage_a(n, 0, False)
        state = stage_b(n - 1, 1, max1, state)
        max1 = stage_a(n + 1, 1, False)
        state = stage_b(n, 0, max0, state)
        return max1, state

    max1, state = _paired_loop(1, qi + 1, trip, (max1, state))
    state = stage_b(2 * qi + 1, 1, max1, state)
    outs = [acc[0:HEAD_DIM, :] / acc[HEAD_DIM:HEAD_DIM + 1, :] for _, acc in state]
    o_ref[0] = jnp.concatenate(outs, axis=0).T.astype(o_ref.dtype)


def _fox_scratch():
    return [pltpu.VMEM((2, 2, ATT_BLOCK, QRY_BLOCK), F32)]


def _post_attn_kernel(x_ref, ysb_ref, yfx_ref, sgsb_ref, sgfx_ref, wbsb_ref, wbfx_ref, wout_ref,
                      gpost_ref, gffn_ref, x1_ref, h2_ref):
    z_sb = _dot(ysb_ref[0], wbsb_ref[...])
    z_fx = _dot(yfx_ref[0], wbfx_ref[...])
    mixed = sgsb_ref[0].astype(F32) * z_sb + sgfx_ref[0].astype(F32) * z_fx
    x1 = x_ref[0] + _rms(_dot(mixed.astype(BF16), wout_ref[...]), gpost_ref[...])
    x1_ref[0] = x1
    h2_ref[0] = _rms(x1, gffn_ref[...]).astype(BF16)


def _post_attn(x, y_sb, y_fx, sg_sb, sg_fx, w_bsb, w_bfx, w_out, g_post, g_ffn, tm):
    b, s, d = x.shape
    tok = lambda w: pl.BlockSpec((1, tm, w), lambda bi, i: (bi, i, 0))
    return pl.pallas_call(
        _post_attn_kernel,
        grid=(b, s // tm),
        in_specs=[tok(d), tok(WIDTH), tok(WIDTH), tok(d), tok(d),
                  _const_spec((WIDTH, d)), _const_spec((WIDTH, d)), _const_spec((d, d)),
                  _const_spec((1, d)), _const_spec((1, d))],
        out_specs=(tok(d), tok(d)),
        out_shape=(jax.ShapeDtypeStruct((b, s, d), F32), jax.ShapeDtypeStruct((b, s, d), BF16)),
        compiler_params=pltpu.CompilerParams(
            dimension_semantics=("arbitrary", "arbitrary"), vmem_limit_bytes=VMEM_LIMIT),
        name="post_attn",
    )(x, y_sb, y_fx, sg_sb, sg_fx, w_bsb.astype(BF16), w_bfx.astype(BF16), w_out.astype(BF16),
      g_post.reshape(1, d).astype(F32), g_ffn.reshape(1, d).astype(F32))


def _gelu_tanh(u):
    return 0.5 * u * (1.0 + jnp.tanh(0.7978845608028654 * (u + 0.044715 * (u * u * u))))


def _ffn_ple_kernel(x1_ref, h2_ref, halo_ref, p_ref, wup_ref, cw_ref, cb_ref, wdn_ref,
                    gpost_ref, wpg_ref, wple_ref, o_ref, u0_ref, u1_ref):
    tm = h2_ref.shape[1]
    fc = FF_CHUNK
    n_chunks = D_FF // fc
    halo = jnp.where(pl.program_id(1) > 0, halo_ref[0], jnp.zeros_like(halo_ref[0]))
    h_ext = jnp.concatenate([halo, h2_ref[0]], axis=0)
    u_refs = (u0_ref, u1_ref)

    def up_project(c):
        for half in range(2):
            cols = slice(half * D_FF + c * fc, half * D_FF + (c + 1) * fc)
            u_refs[c % 2][half] = _dot(h_ext, wup_ref[:, cols])

    def conv(c, half):
        cols = slice(half * D_FF + c * fc, half * D_FF + (c + 1) * fc)
        u_ref = u_refs[c % 2]
        out = cb_ref[:, cols]
        for tap in range(CONV_WIDTH):
            shift = CONV_WIDTH - 1 - tap
            out = out + cw_ref[tap:tap + 1, cols] * u_ref[half, HALO - shift:HALO - shift + tm, :]
        return out

    acc = jnp.zeros((tm, o_ref.shape[2]), F32)
    up_project(0)
    for c in range(n_chunks):
        if c + 1 < n_chunks:
            up_project(c + 1)
        act = _gelu_tanh(conv(c, 0)) * conv(c, 1)
        acc = acc + _dot(act.astype(BF16), wdn_ref[c * fc:(c + 1) * fc, :])

    x2 = x1_ref[0] + _rms(acc, gpost_ref[...])
    gate = _sigmoid(_dot(x2.astype(BF16), wpg_ref[...]))
    ple = _dot(p_ref[0].astype(BF16), wple_ref[...])
    o_ref[0] = x2 + gate * ple


def _ffn_ple(x1, h2, p, w_up, conv_w, conv_b, w_down, g_post, w_ple, w_ple_gate, tm):
    b, s, d = x1.shape
    assert D_FF % FF_CHUNK == 0 and w_up.shape == (d, 2 * D_FF)
    ple_dim = p.shape[-1]
    tok = lambda w: pl.BlockSpec((1, tm, w), lambda bi, i: (bi, i, 0))
    halo_spec = pl.BlockSpec(
        (1, HALO, d), lambda bi, i: (bi, jnp.maximum(i * (tm // HALO) - 1, 0), 0))
    u_scratch = pltpu.VMEM((2, HALO + tm, FF_CHUNK), F32)
    return pl.pallas_call(
        _ffn_ple_kernel,
        grid=(b, s // tm),
        in_specs=[tok(d), tok(d), halo_spec, tok(ple_dim),
                  _const_spec((d, 2 * D_FF)), _const_spec((CONV_WIDTH, 2 * D_FF)),
                  _const_spec((1, 2 * D_FF)), _const_spec((D_FF, d)), _const_spec((1, d)),
                  _const_spec((d, d)), _const_spec((ple_dim, d))],
        out_specs=tok(d),
        out_shape=jax.ShapeDtypeStruct((b, s, d), F32),
        scratch_shapes=[u_scratch, u_scratch],
        compiler_params=pltpu.CompilerParams(
            dimension_semantics=("arbitrary", "arbitrary"), vmem_limit_bytes=VMEM_LIMIT),
        name="ffn_ple",
    )(x1, h2, h2, p, w_up.astype(BF16), conv_w.astype(F32), conv_b.reshape(1, 2 * D_FF).astype(F32),
      w_down.astype(BF16), g_post.reshape(1, d).astype(F32),
      w_ple_gate.astype(BF16), w_ple.astype(BF16))


def _layer(x, p, norm_attn_pre, norm_attn_post, w_in, b_forget, w_branch_sb, w_branch_fox, w_out,
           norm_ffn_pre, norm_ffn_post, w_up, conv_w, conv_b, w_down, w_ple, w_ple_gate):
    qt_sb, k_sb, vt_sb, qt_fx, qa_fx, k_fx, ka_fx, vt_fx, sg_sb, sg_fx = _in_proj(
        x, norm_attn_pre, w_in, b_forget, tm=256)
    y_sb = _attn_call(_sb_attn_kernel, "sb_attn", [qt_sb], [k_sb], vt_sb, _sb_scratch())
    y_fx = _attn_call(_fox_attn_kernel, "fox_attn", [qt_fx, qa_fx], [k_fx, ka_fx], vt_fx,
                      _fox_scratch())
    x1, h2 = _post_attn(x, y_sb, y_fx, sg_sb, sg_fx, w_branch_sb, w_branch_fox, w_out,
                        norm_attn_post, norm_ffn_pre, tm=512)
    return _ffn_ple(x1, h2, p, w_up, conv_w, conv_b, w_down, norm_ffn_post, w_ple, w_ple_gate,
                    tm=512)


def kernel(x, p, norm_attn_pre, norm_attn_post, w_in, b_forget, w_branch_sb, w_branch_fox, w_out,
           norm_ffn_pre, norm_ffn_post, w_up, conv_w, conv_b, w_down, w_ple, w_ple_gate):
    depth = w_in.shape[0]
    for i in range(depth):
        x = _layer(x, p[i], norm_attn_pre[i], norm_attn_post[i], w_in[i], b_forget[i],
                   w_branch_sb[i], w_branch_fox[i], w_out[i], norm_ffn_pre[i], norm_ffn_post[i],
                   w_up[i], conv_w[i], conv_b[i], w_down[i], w_ple[i], w_ple_gate[i])
    return x
```

```python
import functools

import numpy as np
import jax
import jax.numpy as jnp
from jax import lax
from jax.experimental import pallas as pl
from jax.experimental.pallas import tpu as pltpu

HEAD_DIM = 64
N_HEADS = 8
WIDTH = N_HEADS * HEAD_DIM
AUG_PIECES = 3
AUG_ROWS = 8
D_FF = 2816
CONV_WIDTH = 3
RMS_EPS = 1e-6
SCALE = HEAD_DIM ** -0.5

ATT_BLOCK = 256
QRY_BLOCK = 512
FF_CHUNK = 256
HALO = 16
NEG_BIG = -1e30
MAX_INIT = -1e29
LOG2E = 1.4426950408889634

VMEM_LIMIT = 56 * 1024 * 1024

F32 = jnp.float32
BF16 = jnp.bfloat16


def _dot(a, b):
    return jnp.dot(a, b, preferred_element_type=F32)


def _rms(x, g):
    ms = jnp.mean(x * x, axis=-1, keepdims=True)
    return x * lax.rsqrt(ms + RMS_EPS) * g


def _log_sigmoid(z):
    return jnp.minimum(z, 0.0) - jnp.log(1.0 + jnp.exp(-jnp.abs(z)))


def _sigmoid(z):
    return 1.0 / (1.0 + jnp.exp(-z))


def _split3(c):
    c1 = c.astype(BF16)
    r = c - c1.astype(F32)
    c2 = r.astype(BF16)
    c3 = (r - c2.astype(F32)).astype(BF16)
    return c1, c2, c3


def _const_spec(shape):
    return pl.BlockSpec(shape, lambda *_: (0,) * len(shape), pipeline_mode=pl.Buffered(1))


def _in_proj_kernel(x_ref, g_ref, wqkv_ref, wf_ref, bf_ref, wg_ref,
                    selq_ref, selk_ref, oneq_ref, onek_ref,
                    qt_sb_ref, k_sb_ref, vt_sb_ref, qt_fx_ref, qa_fx_ref, k_fx_ref, ka_fx_ref,
                    vt_fx_ref, sg_sb_ref, sg_fx_ref, carry_ref):
    tm = x_ref.shape[1]

    @pl.when(pl.program_id(1) == 0)
    def _():
        carry_ref[...] = jnp.zeros_like(carry_ref)

    h = _rms(x_ref[0], g_ref[...]).astype(BF16)

    logf = _log_sigmoid(_dot(h, wf_ref[...]) + bf_ref[...])
    row = lax.broadcasted_iota(jnp.int32, (tm, tm), 0)
    col = lax.broadcasted_iota(jnp.int32, (tm, tm), 1)
    tri = (col <= row).astype(BF16)
    f1, f2, f3 = _split3(logf)
    c = _dot(tri, f1) + _dot(tri, f2) + _dot(tri, f3) + carry_ref[...]
    carry_ref[...] = c[tm - 1:tm, :]
    c1, c2, c3 = _split3(c * LOG2E)
    lane = lax.broadcasted_iota(jnp.int32, c.shape, 1)
    pieces = jnp.where(lane < N_HEADS, c1.astype(F32),
                       jnp.where(lane < 2 * N_HEADS, c2.astype(F32), c3.astype(F32))).astype(BF16)

    def project(j):
        return _dot(h, wqkv_ref[:, j * WIDTH:(j + 1) * WIDTH])

    def store_transposed_blocks(ref, t):
        for j in range(tm // ATT_BLOCK):
            ref[0, j] = t[:, j * ATT_BLOCK:(j + 1) * ATT_BLOCK].astype(BF16)

    qt_sb_ref[0, 0] = (project(0) * (SCALE * LOG2E)).T.astype(BF16)
    k_sb_ref[0] = project(1).astype(BF16)
    store_transposed_blocks(vt_sb_ref, project(2).T)

    qt_fx_ref[0, 0] = (project(3) * (SCALE * LOG2E)).T.astype(BF16)
    k_fx_ref[0] = project(4).astype(BF16)
    store_transposed_blocks(vt_fx_ref, project(5).T)
    qa_fx_ref[0, 0] = (_dot(pieces, selq_ref[...]) + oneq_ref[...]).T.astype(BF16)
    ka_fx_ref[0] = (_dot(pieces, selk_ref[...]) + onek_ref[...]).astype(BF16)

    d = sg_sb_ref.shape[2]
    sg_sb_ref[0] = _sigmoid(_dot(h, wg_ref[:, 0:d])).astype(BF16)
    sg_fx_ref[0] = _sigmoid(_dot(h, wg_ref[:, d:2 * d])).astype(BF16)


def _in_proj(x, g_pre, w_in, b_forget, tm):
    b, s, d = x.shape
    nblk = s // ATT_BLOCK
    n_qkv = 6 * WIDTH
    w_qkv = w_in[:, 0:n_qkv].astype(BF16)
    w_f = w_in[:, n_qkv:n_qkv + N_HEADS]
    pad = 128 - AUG_PIECES * N_HEADS
    w_f = jnp.pad(jnp.tile(w_f, (1, AUG_PIECES)), ((0, 0), (0, pad))).astype(BF16)
    b_f = jnp.pad(jnp.tile(b_forget.astype(F32), AUG_PIECES), (0, pad)).reshape(1, 128)
    w_g = w_in[:, n_qkv + N_HEADS:].astype(BF16)

    selq = np.zeros((128, WIDTH), np.float32)
    selk = np.zeros((128, WIDTH), np.float32)
    oneq = np.zeros((1, WIDTH), np.float32)
    onek = np.zeros((1, WIDTH), np.float32)
    for hd in range(N_HEADS):
        base = (hd // 2) * 128 + (hd % 2) * AUG_ROWS
        for i in range(AUG_PIECES):
            selk[i * N_HEADS + hd, base + i] = -1.0
            onek[0, base + AUG_PIECES + i] = 1.0
            oneq[0, base + i] = 1.0
            selq[i * N_HEADS + hd, base + AUG_PIECES + i] = 1.0
    selq = jnp.asarray(selq, BF16)
    selk = jnp.asarray(selk, BF16)

    nj = tm // ATT_BLOCK
    per_q = QRY_BLOCK // tm
    rows_t = jax.ShapeDtypeStruct((b, s // QRY_BLOCK, WIDTH, QRY_BLOCK), BF16)
    rows_n = jax.ShapeDtypeStruct((b, s, WIDTH), BF16)
    v_blocks = jax.ShapeDtypeStruct((b, nblk, WIDTH, ATT_BLOCK), BF16)
    gate = jax.ShapeDtypeStruct((b, s, d), BF16)
    spec_t = pl.BlockSpec((1, 1, WIDTH, tm), lambda bi, i: (bi, i // per_q, 0, i % per_q))
    spec_n = pl.BlockSpec((1, tm, WIDTH), lambda bi, i: (bi, i, 0))
    spec_v = pl.BlockSpec((1, nj, WIDTH, ATT_BLOCK), lambda bi, i: (bi, i, 0, 0))
    spec_g = pl.BlockSpec((1, tm, d), lambda bi, i: (bi, i, 0))
    out_shape = (rows_t, rows_n, v_blocks, rows_t, rows_t, rows_n, rows_n, v_blocks, gate, gate)
    out_specs = (spec_t, spec_n, spec_v, spec_t, spec_t, spec_n, spec_n, spec_v, spec_g, spec_g)
    in_specs = [
        pl.BlockSpec((1, tm, d), lambda bi, i: (bi, i, 0)),
        _const_spec((1, d)),
        _const_spec(w_qkv.shape), _const_spec(w_f.shape), _const_spec(b_f.shape),
        _const_spec(w_g.shape), _const_spec(selq.shape), _const_spec(selk.shape),
        _const_spec(oneq.shape), _const_spec(onek.shape),
    ]
    return pl.pallas_call(
        _in_proj_kernel,
        grid=(b, s // tm),
        in_specs=in_specs,
        out_specs=out_specs,
        out_shape=out_shape,
        scratch_shapes=[pltpu.VMEM((1, 128), F32)],
        compiler_params=pltpu.CompilerParams(
            dimension_semantics=("arbitrary", "arbitrary"), vmem_limit_bytes=VMEM_LIMIT),
        name="in_proj",
    )(x, g_pre.reshape(1, d).astype(F32), w_qkv, w_f, b_f, w_g,
      selq, selk, jnp.asarray(oneq), jnp.asarray(onek))


def _diag_offsets():
    tk, tq = ATT_BLOCK, QRY_BLOCK
    offsets = (lax.broadcasted_iota(jnp.int32, (tk, tq), 0)
               - lax.broadcasted_iota(jnp.int32, (tk, tq), 1))
    return offsets, lambda n: (n + 1 - tq // tk) * tk


def _paired_loop(first, stop, trip, carried):
    pairs = lax.shift_right_logical(stop - first, 1)

    def two_trips(i, c):
        p = first + 2 * i
        return trip(p + 1, trip(p, c))

    carried = lax.fori_loop(0, pairs, two_trips, carried)
    return lax.fori_loop(first + 2 * pairs, stop, trip, carried)


def _query_block_loop(n_query_blocks, boundary, inner):
    _, carried = boundary(None, 0)

    def step(qi, carried):
        tiles, started = boundary((qi - 1, carried), qi)
        return inner(qi, tiles, started)

    carried = lax.fori_loop(1, n_query_blocks, step, carried)
    boundary((n_query_blocks - 1, carried), None)


def _neg_abs(z):
    bits = lax.bitcast_convert_type(z, jnp.uint32) | jnp.uint32(0x80000000)
    return lax.bitcast_convert_type(bits, F32)


def _sb_attn_kernel(qt_ref, k_ref, vt_ref, o_ref, lr_ref, lb_ref, t_ref):
    tk, tq = ATT_BLOCK, QRY_BLOCK
    per_q = tq // tk
    assert per_q == 2, "the two scratch slots assume an even number of key blocks per query block"
    offsets, shift = _diag_offsets()
    row = lax.broadcasted_iota(jnp.int32, (tk, tk), 0)
    col = lax.broadcasted_iota(jnp.int32, (tk, tk), 1)
    suffix = (col > row).astype(BF16)

    def last_key_block(qi):
        return (qi + 1) * per_q - 1

    def query_tiles(qi):
        qt = qt_ref[0, qi]
        head_row = lax.broadcasted_iota(jnp.int32, qt.shape, 0)
        zero = jnp.zeros_like(qt)
        return (jnp.where(head_row < HEAD_DIM, qt, zero),
                jnp.where(head_row >= HEAD_DIM, qt, zero))

    def stage_a(last, qth, n, slot, diag):
        kj = last - n
        kblk = k_ref[0, pl.ds(pl.multiple_of(kj * tk, tk), tk), :]
        if diag:
            causal = offsets < shift(n)
        first_rows = []
        for hh in range(2):
            z = _dot(kblk, qth[hh])
            log_beta = jnp.minimum(z, 0.0) - jnp.log2(1.0 + jnp.exp2(_neg_abs(z)))
            log_rest = log_beta - z
            if diag:
                log_rest = jnp.where(causal, log_rest, 0.0)
                log_beta = jnp.where(causal, log_beta, NEG_BIG)
            lr_ref[slot, hh] = log_rest.astype(BF16)
            lb_ref[slot, hh] = log_beta
            first_rows.append(log_rest[0:1, :])
        return tuple(first_rows)

    def stage_b(slot, first_rows):
        col_sums = []
        for hh in range(2):
            within = _dot(suffix, lr_ref[slot, hh])
            t_ref[slot, hh] = lb_ref[slot, hh] + within
            col_sums.append(within[0:1, :] + first_rows[hh])
        return tuple(col_sums)

    def weights(slot, state):
        return tuple(jnp.exp2(t_ref[slot, hh] + state[hh][0]).astype(BF16) for hh in range(2))

    def stage_c(last, n, slot, col_sums, state):
        kj = last - n
        w = weights(slot, state)
        new_state = []
        for hh in range(2):
            carry, acc = state[hh]
            vt = vt_ref[0, kj, hh * HEAD_DIM:(hh + 1) * HEAD_DIM, :]
            new_state.append((carry + col_sums[hh], acc + _dot(vt, w[hh])))
        return tuple(new_state)

    def inner(qi, qth, carried):
        last = last_key_block(qi)

        def trip(p, carried):
            rows1, sums0, state = carried
            n = 2 * p
            rows0 = stage_a(last, qth, n, 0, False)
            state = stage_c(last, n - 2, 0, sums0, state)
            sums1 = stage_b(1, rows1)
            rows1 = stage_a(last, qth, n + 1, 1, False)
            state = stage_c(last, n - 1, 1, sums1, state)
            sums0 = stage_b(0, rows0)
            return rows1, sums0, state

        return _paired_loop(1, qi + 1, trip, carried)

    def boundary(old, new):
        if new is not None:
            qth, last_new = query_tiles(new), last_key_block(new)
            rows0 = stage_a(last_new, qth, 0, 0, True)
        if old is not None:
            qi_old, (rows1_old, sums0_old, state_old) = old
            last_old = last_key_block(qi_old)
            sums1_old = stage_b(1, rows1_old)
        if new is not None:
            rows1 = stage_a(last_new, qth, 1, 1, True)
        if old is not None:
            state_old = stage_c(last_old, last_old - 1, 0, sums0_old, state_old)
            state_old = stage_c(last_old, last_old, 1, sums1_old, state_old)
        if new is not None:
            sums0 = stage_b(0, rows0)
        if old is not None:
            y = jnp.concatenate([st[1] for st in state_old], axis=0).T
            o_ref[0, pl.ds(pl.multiple_of(qi_old * tq, tq), tq), :] = y.astype(o_ref.dtype)
        if new is None:
            return None
        state = tuple((jnp.zeros((1, tq), F32), jnp.zeros((HEAD_DIM, tq), F32)) for _ in range(2))
        return qth, (rows1, sums0, state)

    _query_block_loop(qt_ref.shape[1], boundary, inner)


def _sb_scratch():
    tk, tq = ATT_BLOCK, QRY_BLOCK
    return [pltpu.VMEM((2, 2, tk, tq), BF16),
            pltpu.VMEM((2, 2, tk, tq), F32), pltpu.VMEM((2, 2, tk, tq), F32)]


def _attn_call(kernel_fn, name, qts, ks, vt, scratch):
    b, nq, _, tq = qts[0].shape
    tk = ATT_BLOCK
    s = nq * tq
    pair = 2 * HEAD_DIM
    qt_spec = pl.BlockSpec((1, nq, pair, tq), lambda bi, hp: (bi, 0, hp, 0))
    k_spec = pl.BlockSpec((1, s, pair), lambda bi, hp: (bi, 0, hp))
    vt_spec = pl.BlockSpec((1, s // tk, pair, tk), lambda bi, hp: (bi, 0, hp, 0))
    return pl.pallas_call(
        kernel_fn,
        grid=(b, N_HEADS // 2),
        in_specs=[qt_spec] * len(qts) + [k_spec] * len(ks) + [vt_spec],
        out_specs=pl.BlockSpec((1, s, pair), lambda bi, hp: (bi, 0, hp)),
        out_shape=jax.ShapeDtypeStruct((b, s, WIDTH), BF16),
        scratch_shapes=scratch,
        compiler_params=pltpu.CompilerParams(
            dimension_semantics=("arbitrary", "arbitrary"), vmem_limit_bytes=VMEM_LIMIT),
        name=name,
    )(*qts, *ks, vt)


def _fox_attn_kernel(qt_ref, qa_ref, k_ref, ka_ref, vt_ref, o_ref, s_ref):
    tk, tq = ATT_BLOCK, QRY_BLOCK
    per_q = tq // tk
    assert per_q == 2, "the two scratch slots assume an even number of key blocks per query block"
    offsets, shift = _diag_offsets()
    ones = jnp.ones((HALO, tk), BF16)

    def last_key_block(qi):
        return (qi + 1) * per_q - 1

    def query_tiles(qi):
        qt, qa = qt_ref[0, qi], qa_ref[0, qi]
        row = lax.broadcasted_iota(jnp.int32, qt.shape, 0)
        zero = jnp.zeros_like(qt)
        return tuple(
            jnp.concatenate(
                [jnp.where((row >= hh * HEAD_DIM) & (row < (hh + 1) * HEAD_DIM), qt, zero),
                 jnp.where((row >= hh * AUG_ROWS) & (row < (hh + 1) * AUG_ROWS), qa, zero)], axis=0)
            for hh in range(2))

    def stage_a(last, qth, n, slot, diag):
        kj = last - n
        rows = pl.ds(pl.multiple_of(kj * tk, tk), tk)
        if diag:
            causal = offsets <= shift(n)
        kblk = jnp.concatenate([k_ref[0, rows, :], ka_ref[0, rows, :]], axis=1)
        maxes = []
        for hh in range(2):
            logits = _dot(kblk, qth[hh])
            if diag:
                logits = jnp.where(causal, logits, NEG_BIG)
            s_ref[slot, hh] = logits
            maxes.append(jnp.max(logits, axis=0, keepdims=True))
        return tuple(maxes)

    def stage_b(last, n, slot, maxes, state):
        kj = last - n
        new_state = []
        for hh in range(2):
            m, acc = state[hh]
            m_new = jnp.maximum(m, maxes[hh])
            p = jnp.exp2(s_ref[slot, hh] - m_new)
            vt = vt_ref[0, kj, hh * HEAD_DIM:(hh + 1) * HEAD_DIM, :]
            vt_aug = jnp.concatenate([vt, ones], axis=0)
            acc = acc * jnp.exp2(m - m_new) + _dot(vt_aug, p.astype(BF16))
            new_state.append((m_new, acc))
        return tuple(new_state)

    def inner(qi, qth, carried):
        last = last_key_block(qi)

        def trip(p, carried):
            max1, state = carried
            n = 2 * p
            max0 = stage_a(last, qth, n, 0, False)
            state = stage_b(last, n - 1, 1, max1, state)
            max1 = stage_a(last, qth, n + 1, 1, False)
            state = stage_b(last, n, 0, max0, state)
            return max1, state

        return _paired_loop(1, qi + 1, trip, carried)

    def boundary(old, new):
        if new is not None:
            qth, last_new = query_tiles(new), last_key_block(new)
            max0 = stage_a(last_new, qth, 0, 0, True)
        if old is not None:
            qi_old, (max1_old, state_old) = old
            last_old = last_key_block(qi_old)
            state_old = stage_b(last_old, last_old, 1, max1_old, state_old)
        if new is not None:
            max1 = stage_a(last_new, qth, 1, 1, True)
            state = tuple((jnp.full((1, tq), MAX_INIT, F32), jnp.zeros((HEAD_DIM + HALO, tq), F32))
                          for _ in range(2))
            state = stage_b(last_new, 0, 0, max0, state)
        if old is not None:
            outs = [acc[0:HEAD_DIM, :] / acc[HEAD_DIM:HEAD_DIM + 1, :] for _, acc in state_old]
            y = jnp.concatenate(outs, axis=0).T
            o_ref[0, pl.ds(pl.multiple_of(qi_old * tq, tq), tq), :] = y.astype(o_ref.dtype)
        if new is None:
            return None
        return qth, (max1, state)

    _query_block_loop(qt_ref.shape[1], boundary, inner)


def _fox_scratch():
    return [pltpu.VMEM((2, 2, ATT_BLOCK, QRY_BLOCK), F32)]


def _post_attn_kernel(x_ref, ysb_ref, yfx_ref, sgsb_ref, sgfx_ref, wbsb_ref, wbfx_ref, wout_ref,
                      gpost_ref, gffn_ref, x1_ref, h2_ref):
    z_sb = _dot(ysb_ref[0], wbsb_ref[...])
    z_fx = _dot(yfx_ref[0], wbfx_ref[...])
    mixed = sgsb_ref[0].astype(F32) * z_sb + sgfx_ref[0].astype(F32) * z_fx
    x1 = x_ref[0] + _rms(_dot(mixed.astype(BF16), wout_ref[...]), gpost_ref[...])
    x1_ref[0] = x1
    h2_ref[0] = _rms(x1, gffn_ref[...]).astype(BF16)


def _post_attn(x, y_sb, y_fx, sg_sb, sg_fx, w_bsb, w_bfx, w_out, g_post, g_ffn, tm):
    b, s, d = x.shape
    tok = lambda w: pl.BlockSpec((1, tm, w), lambda bi, i: (bi, i, 0))
    return pl.pallas_call(
        _post_attn_kernel,
        grid=(b, s // tm),
        in_specs=[tok(d), tok(WIDTH), tok(WIDTH), tok(d), tok(d),
                  _const_spec((WIDTH, d)), _const_spec((WIDTH, d)), _const_spec((d, d)),
                  _const_spec((1, d)), _const_spec((1, d))],
        out_specs=(tok(d), tok(d)),
        out_shape=(jax.ShapeDtypeStruct((b, s, d), F32), jax.ShapeDtypeStruct((b, s, d), BF16)),
        compiler_params=pltpu.CompilerParams(
            dimension_semantics=("arbitrary", "arbitrary"), vmem_limit_bytes=VMEM_LIMIT),
        name="post_attn",
    )(x, y_sb, y_fx, sg_sb, sg_fx, w_bsb.astype(BF16), w_bfx.astype(BF16), w_out.astype(BF16),
      g_post.reshape(1, d).astype(F32), g_ffn.reshape(1, d).astype(F32))


def _gelu_tanh(u):
    return 0.5 * u * (1.0 + jnp.tanh(0.7978845608028654 * (u + 0.044715 * (u * u * u))))


def _ffn_ple_kernel(x1_ref, h2_ref, halo_ref, p_ref, wup_ref, cw_ref, cb_ref, wdn_ref,
                    gpost_ref, wpg_ref, wple_ref, o_ref, u0_ref, u1_ref):
    tm = h2_ref.shape[1]
    fc = FF_CHUNK
    n_chunks = D_FF // fc
    halo = jnp.where(pl.program_id(1) > 0, halo_ref[0], jnp.zeros_like(halo_ref[0]))
    h_ext = jnp.concatenate([halo, h2_ref[0]], axis=0)
    u_refs = (u0_ref, u1_ref)

    def up_project(c):
        for half in range(2):
            cols = slice(half * D_FF + c * fc, half * D_FF + (c + 1) * fc)
            u_refs[c % 2][half] = _dot(h_ext, wup_ref[:, cols])

    def conv(c, half):
        cols = slice(half * D_FF + c * fc, half * D_FF + (c + 1) * fc)
        u_ref = u_refs[c % 2]
        out = cb_ref[:, cols]
        for tap in range(CONV_WIDTH):
            shift = CONV_WIDTH - 1 - tap
            out = out + cw_ref[tap:tap + 1, cols] * u_ref[half, HALO - shift:HALO - shift + tm, :]
        return out

    acc = jnp.zeros((tm, o_ref.shape[2]), F32)
    up_project(0)
    for c in range(n_chunks):
        if c + 1 < n_chunks:
            up_project(c + 1)
        act = _gelu_tanh(conv(c, 0)) * conv(c, 1)
        acc = acc + _dot(act.astype(BF16), wdn_ref[c * fc:(c + 1) * fc, :])

    x2 = x1_ref[0] + _rms(acc, gpost_ref[...])
    gate = _sigmoid(_dot(x2.astype(BF16), wpg_ref[...]))
    ple = _dot(p_ref[0].astype(BF16), wple_ref[...])
    o_ref[0] = x2 + gate * ple


def _ffn_ple(x1, h2, p, w_up, conv_w, conv_b, w_down, g_post, w_ple, w_ple_gate, tm):
    b, s, d = x1.shape
    assert D_FF % FF_CHUNK == 0 and w_up.shape == (d, 2 * D_FF)
    ple_dim = p.shape[-1]
    tok = lambda w: pl.BlockSpec((1, tm, w), lambda bi, i: (bi, i, 0))
    halo_spec = pl.BlockSpec(
        (1, HALO, d), lambda bi, i: (bi, jnp.maximum(i * (tm // HALO) - 1, 0), 0))
    u_scratch = pltpu.VMEM((2, HALO + tm, FF_CHUNK), F32)
    return pl.pallas_call(
        _ffn_ple_kernel,
        grid=(b, s // tm),
        in_specs=[tok(d), tok(d), halo_spec, tok(ple_dim),
                  _const_spec((d, 2 * D_FF)), _const_spec((CONV_WIDTH, 2 * D_FF)),
                  _const_spec((1, 2 * D_FF)), _const_spec((D_FF, d)), _const_spec((1, d)),
                  _const_spec((d, d)), _const_spec((ple_dim, d))],
        out_specs=tok(d),
        out_shape=jax.ShapeDtypeStruct((b, s, d), F32),
        scratch_shapes=[u_scratch, u_scratch],
        compiler_params=pltpu.CompilerParams(
            dimension_semantics=("arbitrary", "arbitrary"), vmem_limit_bytes=VMEM_LIMIT),
        name="ffn_ple",
    )(x1, h2, h2, p, w_up.astype(BF16), conv_w.astype(F32), conv_b.reshape(1, 2 * D_FF).astype(F32),
      w_down.astype(BF16), g_post.reshape(1, d).astype(F32),
      w_ple_gate.astype(BF16), w_ple.astype(BF16))


def _layer(x, p, norm_attn_pre, norm_attn_post, w_in, b_forget, w_branch_sb, w_branch_fox, w_out,
           norm_ffn_pre, norm_ffn_post, w_up, conv_w, conv_b, w_down, w_ple, w_ple_gate):
    qt_sb, k_sb, vt_sb, qt_fx, qa_fx, k_fx, ka_fx, vt_fx, sg_sb, sg_fx = _in_proj(
        x, norm_attn_pre, w_in, b_forget, tm=256)
    y_sb = _attn_call(_sb_attn_kernel, "sb_attn", [qt_sb], [k_sb], vt_sb, _sb_scratch())
    y_fx = _attn_call(_fox_attn_kernel, "fox_attn", [qt_fx, qa_fx], [k_fx, ka_fx], vt_fx,
                      _fox_scratch())
    x1, h2 = _post_attn(x, y_sb, y_fx, sg_sb, sg_fx, w_branch_sb, w_branch_fox, w_out,
                        norm_attn_post, norm_ffn_pre, tm=512)
    return _ffn_ple(x1, h2, p, w_up, conv_w, conv_b, w_down, norm_ffn_post, w_ple, w_ple_gate,
                    tm=512)


def kernel(x, p, norm_attn_pre, norm_attn_post, w_in, b_forget, w_branch_sb, w_branch_fox, w_out,
           norm_ffn_pre, norm_ffn_post, w_up, conv_w, conv_b, w_down, w_ple, w_ple_gate):
    depth = w_in.shape[0]
    for i in range(depth):
        x = _layer(x, p[i], norm_attn_pre[i], norm_attn_post[i], w_in[i], b_forget[i],
                   w_branch_sb[i], w_branch_fox[i], w_out[i], norm_ffn_pre[i], norm_ffn_post[i],
                   w_up[i], conv_w[i], conv_b[i], w_down[i], w_ple[i], w_ple_gate[i])
    return x
```

```python
import functools

import numpy as np
import jax
import jax.numpy as jnp
from jax import lax
from jax.experimental import pallas as pl
from jax.experimental.pallas import tpu as pltpu

HEAD_DIM = 64
N_HEADS = 8
WIDTH = N_HEADS * HEAD_DIM
AUG_PIECES = 3
AUG_ROWS = 8
D_FF = 2816
CONV_WIDTH = 3
RMS_EPS = 1e-6
SCALE = HEAD_DIM ** -0.5

ATT_BLOCK = 256
QRY_BLOCK = 512
FF_CHUNK = 256
FF_GROUPS = 2
HALO = 16
NEG_BIG = -1e30
MAX_INIT = -1e29
LOG2E = 1.4426950408889634

VMEM_LIMIT = 56 * 1024 * 1024

F32 = jnp.float32
BF16 = jnp.bfloat16


def _dot(a, b):
    return jnp.dot(a, b, preferred_element_type=F32)


def _rms(x, g):
    ms = jnp.mean(x * x, axis=-1, keepdims=True)
    return x * lax.rsqrt(ms + RMS_EPS) * g


def _log_sigmoid(z):
    return jnp.minimum(z, 0.0) - jnp.log(1.0 + jnp.exp(-jnp.abs(z)))


def _sigmoid(z):
    return 1.0 / (1.0 + jnp.exp(-z))


def _split3(c):
    c1 = c.astype(BF16)
    r = c - c1.astype(F32)
    c2 = r.astype(BF16)
    c3 = (r - c2.astype(F32)).astype(BF16)
    return c1, c2, c3


def _const_spec(shape):
    return pl.BlockSpec(shape, lambda *_: (0,) * len(shape), pipeline_mode=pl.Buffered(1))


def _in_proj_kernel(x_ref, g_ref, wqkv_ref, wf_ref, bf_ref, wg_ref,
                    selq_ref, selk_ref, oneq_ref, onek_ref,
                    qt_sb_ref, k_sb_ref, vt_sb_ref, qt_fx_ref, qa_fx_ref, k_fx_ref, ka_fx_ref,
                    vt_fx_ref, sg_sb_ref, sg_fx_ref, carry_ref):
    tm = x_ref.shape[1]

    @pl.when(pl.program_id(1) == 0)
    def _():
        carry_ref[...] = jnp.zeros_like(carry_ref)

    h = _rms(x_ref[0], g_ref[...]).astype(BF16)

    logf = _log_sigmoid(_dot(h, wf_ref[...]) + bf_ref[...])
    row = lax.broadcasted_iota(jnp.int32, (tm, tm), 0)
    col = lax.broadcasted_iota(jnp.int32, (tm, tm), 1)
    tri = (col <= row).astype(BF16)
    f1, f2, f3 = _split3(logf)
    c = _dot(tri, f1) + _dot(tri, f2) + _dot(tri, f3) + carry_ref[...]
    carry_ref[...] = c[tm - 1:tm, :]
    c1, c2, c3 = _split3(c * LOG2E)
    lane = lax.broadcasted_iota(jnp.int32, c.shape, 1)
    pieces = jnp.where(lane < N_HEADS, c1.astype(F32),
                       jnp.where(lane < 2 * N_HEADS, c2.astype(F32), c3.astype(F32))).astype(BF16)

    def project(j):
        return _dot(h, wqkv_ref[:, j * WIDTH:(j + 1) * WIDTH])

    def store_transposed_blocks(ref, t):
        for j in range(tm // ATT_BLOCK):
            ref[0, j] = t[:, j * ATT_BLOCK:(j + 1) * ATT_BLOCK].astype(BF16)

    qt_sb_ref[0, 0] = (project(0) * (SCALE * LOG2E)).T.astype(BF16)
    k_sb_ref[0] = project(1).astype(BF16)
    store_transposed_blocks(vt_sb_ref, project(2).T)

    qt_fx_ref[0, 0] = (project(3) * (SCALE * LOG2E)).T.astype(BF16)
    k_fx_ref[0] = project(4).astype(BF16)
    store_transposed_blocks(vt_fx_ref, project(5).T)
    qa_fx_ref[0, 0] = (_dot(pieces, selq_ref[...]) + oneq_ref[...]).T.astype(BF16)
    ka_fx_ref[0] = (_dot(pieces, selk_ref[...]) + onek_ref[...]).astype(BF16)

    d = sg_sb_ref.shape[2]
    sg_sb_ref[0] = _sigmoid(_dot(h, wg_ref[:, 0:d])).astype(BF16)
    sg_fx_ref[0] = _sigmoid(_dot(h, wg_ref[:, d:2 * d])).astype(BF16)


def _in_proj(x, g_pre, w_in, b_forget, tm):
    b, s, d = x.shape
    nblk = s // ATT_BLOCK
    n_qkv = 6 * WIDTH
    w_qkv = w_in[:, 0:n_qkv].astype(BF16)
    w_f = w_in[:, n_qkv:n_qkv + N_HEADS]
    pad = 128 - AUG_PIECES * N_HEADS
    w_f = jnp.pad(jnp.tile(w_f, (1, AUG_PIECES)), ((0, 0), (0, pad))).astype(BF16)
    b_f = jnp.pad(jnp.tile(b_forget.astype(F32), AUG_PIECES), (0, pad)).reshape(1, 128)
    w_g = w_in[:, n_qkv + N_HEADS:].astype(BF16)

    selq = np.zeros((128, WIDTH), np.float32)
    selk = np.zeros((128, WIDTH), np.float32)
    oneq = np.zeros((1, WIDTH), np.float32)
    onek = np.zeros((1, WIDTH), np.float32)
    for hd in range(N_HEADS):
        base = (hd // 2) * 128 + (hd % 2) * AUG_ROWS
        for i in range(AUG_PIECES):
            selk[i * N_HEADS + hd, base + i] = -1.0
            onek[0, base + AUG_PIECES + i] = 1.0
            oneq[0, base + i] = 1.0
            selq[i * N_HEADS + hd, base + AUG_PIECES + i] = 1.0
    selq = jnp.asarray(selq, BF16)
    selk = jnp.asarray(selk, BF16)

    nj = tm // ATT_BLOCK
    per_q = QRY_BLOCK // tm
    rows_t = jax.ShapeDtypeStruct((b, s // QRY_BLOCK, WIDTH, QRY_BLOCK), BF16)
    rows_n = jax.ShapeDtypeStruct((b, s, WIDTH), BF16)
    v_blocks = jax.ShapeDtypeStruct((b, nblk, WIDTH, ATT_BLOCK), BF16)
    gate = jax.ShapeDtypeStruct((b, s, d), BF16)
    spec_t = pl.BlockSpec((1, 1, WIDTH, tm), lambda bi, i: (bi, i // per_q, 0, i % per_q))
    spec_n = pl.BlockSpec((1, tm, WIDTH), lambda bi, i: (bi, i, 0))
    spec_v = pl.BlockSpec((1, nj, WIDTH, ATT_BLOCK), lambda bi, i: (bi, i, 0, 0))
    spec_g = pl.BlockSpec((1, tm, d), lambda bi, i: (bi, i, 0))
    out_shape = (rows_t, rows_n, v_blocks, rows_t, rows_t, rows_n, rows_n, v_blocks, gate, gate)
    out_specs = (spec_t, spec_n, spec_v, spec_t, spec_t, spec_n, spec_n, spec_v, spec_g, spec_g)
    in_specs = [
        pl.BlockSpec((1, tm, d), lambda bi, i: (bi, i, 0)),
        _const_spec((1, d)),
        _const_spec(w_qkv.shape), _const_spec(w_f.shape), _const_spec(b_f.shape),
        _const_spec(w_g.shape), _const_spec(selq.shape), _const_spec(selk.shape),
        _const_spec(oneq.shape), _const_spec(onek.shape),
    ]
    return pl.pallas_call(
        _in_proj_kernel,
        grid=(b, s // tm),
        in_specs=in_specs,
        out_specs=out_specs,
        out_shape=out_shape,
        scratch_shapes=[pltpu.VMEM((1, 128), F32)],
        compiler_params=pltpu.CompilerParams(
            dimension_semantics=("arbitrary", "arbitrary"), vmem_limit_bytes=VMEM_LIMIT),
        name="in_proj",
    )(x, g_pre.reshape(1, d).astype(F32), w_qkv, w_f, b_f, w_g,
      selq, selk, jnp.asarray(oneq), jnp.asarray(onek))


def _diag_offsets():
    tk, tq = ATT_BLOCK, QRY_BLOCK
    offsets = (lax.broadcasted_iota(jnp.int32, (tk, tq), 0)
               - lax.broadcasted_iota(jnp.int32, (tk, tq), 1))
    return offsets, lambda n: (n + 1 - tq // tk) * tk


def _paired_loop(first, stop, trip, carried, widths=(2, 1)):
    for width in widths:
        shift = width.bit_length() - 1
        assert width == 1 << shift
        count = lax.shift_right_logical(stop - first, shift)

        def body(i, c, width=width, first=first):
            for j in range(width):
                c = trip(first + width * i + j, c)
            return c

        carried = lax.fori_loop(0, count, body, carried)
        first = first + width * count
    return carried


def _query_block_loop(n_query_blocks, boundary, inner):
    _, carried = boundary(None, 0)

    def step(qi, carried):
        tiles, started = boundary((qi - 1, carried), qi)
        return inner(qi, tiles, started)

    carried = lax.fori_loop(1, n_query_blocks, step, carried)
    boundary((n_query_blocks - 1, carried), None)


def _neg_abs(z):
    bits = lax.bitcast_convert_type(z, jnp.uint32) | jnp.uint32(0x80000000)
    return lax.bitcast_convert_type(bits, F32)


def _sb_attn_kernel(qt_ref, k_ref, vt_ref, o_ref, lr_ref, lb_ref, t_ref):
    tk, tq = ATT_BLOCK, QRY_BLOCK
    per_q = tq // tk
    assert per_q == 2, "the two scratch slots assume an even number of key blocks per query block"
    offsets, shift = _diag_offsets()
    row = lax.broadcasted_iota(jnp.int32, (tk, tk), 0)
    col = lax.broadcasted_iota(jnp.int32, (tk, tk), 1)
    suffix = (col > row).astype(BF16)

    def last_key_block(qi):
        return (qi + 1) * per_q - 1

    def query_tiles(qi):
        qt = qt_ref[0, qi]
        head_row = lax.broadcasted_iota(jnp.int32, qt.shape, 0)
        zero = jnp.zeros_like(qt)
        return (jnp.where(head_row < HEAD_DIM, qt, zero),
                jnp.where(head_row >= HEAD_DIM, qt, zero))

    def stage_a(last, qth, n, slot, diag):
        kj = last - n
        kblk = k_ref[0, pl.ds(pl.multiple_of(kj * tk, tk), tk), :]
        if diag:
            causal = offsets < shift(n)
        first_rows = []
        for hh in range(2):
            z = _dot(kblk, qth[hh])
            log_beta = jnp.minimum(z, 0.0) - jnp.log2(1.0 + jnp.exp2(_neg_abs(z)))
            log_rest = log_beta - z
            if diag:
                log_rest = jnp.where(causal, log_rest, 0.0)
                log_beta = jnp.where(causal, log_beta, NEG_BIG)
            lr_ref[slot, hh] = log_rest.astype(BF16)
            lb_ref[slot, hh] = log_beta
            first_rows.append(log_rest[0:1, :])
        return tuple(first_rows)

    def stage_b(slot, first_rows):
        col_sums = []
        for hh in range(2):
            within = _dot(suffix, lr_ref[slot, hh])
            t_ref[slot, hh] = lb_ref[slot, hh] + within
            col_sums.append(within[0:1, :] + first_rows[hh])
        return tuple(col_sums)

    def weights(slot, state):
        return tuple(jnp.exp2(t_ref[slot, hh] + state[hh][0]).astype(BF16) for hh in range(2))

    def stage_c(last, n, slot, col_sums, state):
        kj = last - n
        w = weights(slot, state)
        new_state = []
        for hh in range(2):
            carry, acc = state[hh]
            vt = vt_ref[0, kj, hh * HEAD_DIM:(hh + 1) * HEAD_DIM, :]
            new_state.append((carry + col_sums[hh], acc + _dot(vt, w[hh])))
        return tuple(new_state)

    def inner(qi, qth, carried):
        last = last_key_block(qi)

        def trip(p, carried):
            rows1, sums0, state = carried
            n = 2 * p
            rows0 = stage_a(last, qth, n, 0, False)
            state = stage_c(last, n - 2, 0, sums0, state)
            sums1 = stage_b(1, rows1)
            rows1 = stage_a(last, qth, n + 1, 1, False)
            state = stage_c(last, n - 1, 1, sums1, state)
            sums0 = stage_b(0, rows0)
            return rows1, sums0, state

        return _paired_loop(1, qi + 1, trip, carried, widths=(4, 2, 1))

    def boundary(old, new):
        if new is not None:
            qth, last_new = query_tiles(new), last_key_block(new)
            rows0 = stage_a(last_new, qth, 0, 0, True)
        if old is not None:
            qi_old, (rows1_old, sums0_old, state_old) = old
            last_old = last_key_block(qi_old)
            sums1_old = stage_b(1, rows1_old)
        if new is not None:
            rows1 = stage_a(last_new, qth, 1, 1, True)
        if old is not None:
            state_old = stage_c(last_old, last_old - 1, 0, sums0_old, state_old)
            state_old = stage_c(last_old, last_old, 1, sums1_old, state_old)
        if new is not None:
            sums0 = stage_b(0, rows0)
        if old is not None:
            y = jnp.concatenate([st[1] for st in state_old], axis=0).T
            o_ref[0, pl.ds(pl.multiple_of(qi_old * tq, tq), tq), :] = y.astype(o_ref.dtype)
        if new is None:
            return None
        state = tuple((jnp.zeros((1, tq), F32), jnp.zeros((HEAD_DIM, tq), F32)) for _ in range(2))
        return qth, (rows1, sums0, state)

    _query_block_loop(qt_ref.shape[1], boundary, inner)


def _sb_scratch():
    tk, tq = ATT_BLOCK, QRY_BLOCK
    return [pltpu.VMEM((2, 2, tk, tq), BF16),
            pltpu.VMEM((2, 2, tk, tq), F32), pltpu.VMEM((2, 2, tk, tq), F32)]


def _attn_call(kernel_fn, name, qts, ks, vt, scratch):
    b, nq, _, tq = qts[0].shape
    tk = ATT_BLOCK
    s = nq * tq
    pair = 2 * HEAD_DIM
    qt_spec = pl.BlockSpec((1, nq, pair, tq), lambda bi, hp: (bi, 0, hp, 0))
    k_spec = pl.BlockSpec((1, s, pair), lambda bi, hp: (bi, 0, hp))
    vt_spec = pl.BlockSpec((1, s // tk, pair, tk), lambda bi, hp: (bi, 0, hp, 0))
    return pl.pallas_call(
        kernel_fn,
        grid=(b, N_HEADS // 2),
        in_specs=[qt_spec] * len(qts) + [k_spec] * len(ks) + [vt_spec],
        out_specs=pl.BlockSpec((1, s, pair), lambda bi, hp: (bi, 0, hp)),
        out_shape=jax.ShapeDtypeStruct((b, s, WIDTH), BF16),
        scratch_shapes=scratch,
        compiler_params=pltpu.CompilerParams(
            dimension_semantics=("arbitrary", "arbitrary"), vmem_limit_bytes=VMEM_LIMIT),
        name=name,
    )(*qts, *ks, vt)


def _fox_attn_kernel(qt_ref, qa_ref, k_ref, ka_ref, vt_ref, o_ref, s_ref):
    tk, tq = ATT_BLOCK, QRY_BLOCK
    per_q = tq // tk
    assert per_q == 2, "the two scratch slots assume an even number of key blocks per query block"
    offsets, shift = _diag_offsets()
    ones = jnp.ones((HALO, tk), BF16)

    def last_key_block(qi):
        return (qi + 1) * per_q - 1

    def query_tiles(qi):
        qt, qa = qt_ref[0, qi], qa_ref[0, qi]
        row = lax.broadcasted_iota(jnp.int32, qt.shape, 0)
        zero = jnp.zeros_like(qt)
        return tuple(
            jnp.concatenate(
                [jnp.where((row >= hh * HEAD_DIM) & (row < (hh + 1) * HEAD_DIM), qt, zero),
                 jnp.where((row >= hh * AUG_ROWS) & (row < (hh + 1) * AUG_ROWS), qa, zero)], axis=0)
            for hh in range(2))

    def stage_a(last, qth, n, slot, diag):
        kj = last - n
        rows = pl.ds(pl.multiple_of(kj * tk, tk), tk)
        if diag:
            causal = offsets <= shift(n)
        kblk = jnp.concatenate([k_ref[0, rows, :], ka_ref[0, rows, :]], axis=1)
        maxes = []
        for hh in range(2):
            logits = _dot(kblk, qth[hh])
            if diag:
                logits = jnp.where(causal, logits, NEG_BIG)
            s_ref[slot, hh] = logits
            maxes.append(jnp.max(logits, axis=0, keepdims=True))
        return tuple(maxes)

    def stage_b(last, n, slot, maxes, state):
        kj = last - n
        new_state = []
        for hh in range(2):
            m, acc = state[hh]
            m_new = jnp.maximum(m, maxes[hh])
            p = jnp.exp2(s_ref[slot, hh] - m_new)
            vt = vt_ref[0, kj, hh * HEAD_DIM:(hh + 1) * HEAD_DIM, :]
            vt_aug = jnp.concatenate([vt, ones], axis=0)
            acc = acc * jnp.exp2(m - m_new) + _dot(vt_aug, p.astype(BF16))
            new_state.append((m_new, acc))
        return tuple(new_state)

    def inner(qi, qth, carried):
        last = last_key_block(qi)

        def trip(p, carried):
            max1, state = carried
            n = 2 * p
            max0 = stage_a(last, qth, n, 0, False)
            state = stage_b(last, n - 1, 1, max1, state)
            max1 = stage_a(last, qth, n + 1, 1, False)
            state = stage_b(last, n, 0, max0, state)
            return max1, state

        return _paired_loop(1, qi + 1, trip, carried)

    def boundary(old, new):
        if new is not None:
            qth, last_new = query_tiles(new), last_key_block(new)
            max0 = stage_a(last_new, qth, 0, 0, True)
        if old is not None:
            qi_old, (max1_old, state_old) = old
            last_old = last_key_block(qi_old)
            state_old = stage_b(last_old, last_old, 1, max1_old, state_old)
        if new is not None:
            max1 = stage_a(last_new, qth, 1, 1, True)
            state = tuple((jnp.full((1, tq), MAX_INIT, F32), jnp.zeros((HEAD_DIM + HALO, tq), F32))
                          for _ in range(2))
            state = stage_b(last_new, 0, 0, max0, state)
        if old is not None:
            outs = [acc[0:HEAD_DIM, :] / acc[HEAD_DIM:HEAD_DIM + 1, :] for _, acc in state_old]
            y = jnp.concatenate(outs, axis=0).T
            o_ref[0, pl.ds(pl.multiple_of(qi_old * tq, tq), tq), :] = y.astype(o_ref.dtype)
        if new is None:
            return None
        return qth, (max1, state)

    _query_block_loop(qt_ref.shape[1], boundary, inner)


def _fox_scratch():
    return [pltpu.VMEM((2, 2, ATT_BLOCK, QRY_BLOCK), F32)]


def _post_attn_kernel(x_ref, ysb_ref, yfx_ref, sgsb_ref, sgfx_ref, wbsb_ref, wbfx_ref, wout_ref,
                      gpost_ref, gffn_ref, x1_ref, h2_ref):
    z_sb = _dot(ysb_ref[0], wbsb_ref[...])
    z_fx = _dot(yfx_ref[0], wbfx_ref[...])
    mixed = sgsb_ref[0].astype(F32) * z_sb + sgfx_ref[0].astype(F32) * z_fx
    x1 = x_ref[0] + _rms(_dot(mixed.astype(BF16), wout_ref[...]), gpost_ref[...])
    x1_ref[0] = x1
    h2_ref[0] = _rms(x1, gffn_ref[...]).astype(BF16)


def _post_attn(x, y_sb, y_fx, sg_sb, sg_fx, w_bsb, w_bfx, w_out, g_post, g_ffn, tm):
    b, s, d = x.shape
    tok = lambda w: pl.BlockSpec((1, tm, w), lambda bi, i: (bi, i, 0))
    return pl.pallas_call(
        _post_attn_kernel,
        grid=(b, s // tm),
        in_specs=[tok(d), tok(WIDTH), tok(WIDTH), tok(d), tok(d),
                  _const_spec((WIDTH, d)), _const_spec((WIDTH, d)), _const_spec((d, d)),
                  _const_spec((1, d)), _const_spec((1, d))],
        out_specs=(tok(d), tok(d)),
        out_shape=(jax.ShapeDtypeStruct((b, s, d), F32), jax.ShapeDtypeStruct((b, s, d), BF16)),
        compiler_params=pltpu.CompilerParams(
            dimension_semantics=("arbitrary", "arbitrary"), vmem_limit_bytes=VMEM_LIMIT),
        name="post_attn",
    )(x, y_sb, y_fx, sg_sb, sg_fx, w_bsb.astype(BF16), w_bfx.astype(BF16), w_out.astype(BF16),
      g_post.reshape(1, d).astype(F32), g_ffn.reshape(1, d).astype(F32))


def _gelu_tanh(u):
    return 0.5 * u * (1.0 + jnp.tanh(0.7978845608028654 * (u + 0.044715 * (u * u * u))))


def _ffn_ple_kernel(x1_ref, h2_ref, halo_ref, p_ref, wup_ref, cw_ref, cb_ref, wdn_ref,
                    gpost_ref, wpg_ref, wple_ref, o_ref, u0_ref, u1_ref, act_ref):
    tm = h2_ref.shape[1]
    fc = FF_CHUNK
    n_chunks = D_FF // fc
    halo = jnp.where(pl.program_id(1) > 0, halo_ref[0], jnp.zeros_like(halo_ref[0]))
    h_ext = jnp.concatenate([halo, h2_ref[0]], axis=0)
    u_refs = (u0_ref, u1_ref)

    def up_project(c):
        for half in range(2):
            cols = slice(half * D_FF + c * fc, half * D_FF + (c + 1) * fc)
            u_refs[c % 2][half] = _dot(h_ext, wup_ref[:, cols])

    def conv(c, half):
        cols = slice(half * D_FF + c * fc, half * D_FF + (c + 1) * fc)
        u_ref = u_refs[c % 2]
        out = cb_ref[:, cols]
        for tap in range(CONV_WIDTH):
            shift = CONV_WIDTH - 1 - tap
            out = out + cw_ref[tap:tap + 1, cols] * u_ref[half, HALO - shift:HALO - shift + tm, :]
        return out

    bounds = [round(g * n_chunks / FF_GROUPS) for g in range(FF_GROUPS + 1)]
    acc = None
    up_project(0)
    for c in range(n_chunks):
        if c + 1 < n_chunks:
            up_project(c + 1)
        act = _gelu_tanh(conv(c, 0)) * conv(c, 1)
        act_ref[:, c * fc:(c + 1) * fc] = act.astype(BF16)
        if c + 1 in bounds:
            lo = bounds[bounds.index(c + 1) - 1] * fc
            part = _dot(act_ref[:, lo:(c + 1) * fc], wdn_ref[lo:(c + 1) * fc, :])
            acc = part if acc is None else acc + part

    x2 = x1_ref[0] + _rms(acc, gpost_ref[...])
    gate = _sigmoid(_dot(x2.astype(BF16), wpg_ref[...]))
    ple = _dot(p_ref[0].astype(BF16), wple_ref[...])
    o_ref[0] = x2 + gate * ple


def _ffn_ple(x1, h2, p, w_up, conv_w, conv_b, w_down, g_post, w_ple, w_ple_gate, tm):
    b, s, d = x1.shape
    assert D_FF % FF_CHUNK == 0 and w_up.shape == (d, 2 * D_FF)
    ple_dim = p.shape[-1]
    tok = lambda w: pl.BlockSpec((1, tm, w), lambda bi, i: (bi, i, 0))
    halo_spec = pl.BlockSpec(
        (1, HALO, d), lambda bi, i: (bi, jnp.maximum(i * (tm // HALO) - 1, 0), 0))
    u_scratch = pltpu.VMEM((2, HALO + tm, FF_CHUNK), F32)
    return pl.pallas_call(
        _ffn_ple_kernel,
        grid=(b, s // tm),
        in_specs=[tok(d), tok(d), halo_spec, tok(ple_dim),
                  _const_spec((d, 2 * D_FF)), _const_spec((CONV_WIDTH, 2 * D_FF)),
                  _const_spec((1, 2 * D_FF)), _const_spec((D_FF, d)), _const_spec((1, d)),
                  _const_spec((d, d)), _const_spec((ple_dim, d))],
        out_specs=tok(d),
        out_shape=jax.ShapeDtypeStruct((b, s, d), F32),
        scratch_shapes=[u_scratch, u_scratch, pltpu.VMEM((tm, D_FF), BF16)],
        compiler_params=pltpu.CompilerParams(
            dimension_semantics=("arbitrary", "arbitrary"), vmem_limit_bytes=VMEM_LIMIT),
        name="ffn_ple",
    )(x1, h2, h2, p, w_up.astype(BF16), conv_w.astype(F32), conv_b.reshape(1, 2 * D_FF).astype(F32),
      w_down.astype(BF16), g_post.reshape(1, d).astype(F32),
      w_ple_gate.astype(BF16), w_ple.astype(BF16))


def _layer(x, p, norm_attn_pre, norm_attn_post, w_in, b_forget, w_branch_sb, w_branch_fox, w_out,
           norm_ffn_pre, norm_ffn_post, w_up, conv_w, conv_b, w_down, w_ple, w_ple_gate):
    qt_sb, k_sb, vt_sb, qt_fx, qa_fx, k_fx, ka_fx, vt_fx, sg_sb, sg_fx = _in_proj(
        x, norm_attn_pre, w_in, b_forget, tm=256)
    y_sb = _attn_call(_sb_attn_kernel, "sb_attn", [qt_sb], [k_sb], vt_sb, _sb_scratch())
    y_fx = _attn_call(_fox_attn_kernel, "fox_attn", [qt_fx, qa_fx], [k_fx, ka_fx], vt_fx,
                      _fox_scratch())
    x1, h2 = _post_attn(x, y_sb, y_fx, sg_sb, sg_fx, w_branch_sb, w_branch_fox, w_out,
                        norm_attn_post, norm_ffn_pre, tm=512)
    return _ffn_ple(x1, h2, p, w_up, conv_w, conv_b, w_down, norm_ffn_post, w_ple, w_ple_gate,
                    tm=512)


def kernel(x, p, norm_attn_pre, norm_attn_post, w_in, b_forget, w_branch_sb, w_branch_fox, w_out,
           norm_ffn_pre, norm_ffn_post, w_up, conv_w, conv_b, w_down, w_ple, w_ple_gate):
    depth = w_in.shape[0]
    for i in range(depth):
        x = _layer(x, p[i], norm_attn_pre[i], norm_attn_post[i], w_in[i], b_forget[i],
                   w_branch_sb[i], w_branch_fox[i], w_out[i], norm_ffn_pre[i], norm_ffn_post[i],
                   w_up[i], conv_w[i], conv_b[i], w_down[i], w_ple[i], w_ple_gate[i])
    return x
```
